```python
import jax, jax.numpy as jnp
from jax import lax
import numpy as np

D_MODEL = 1024
BATCH = 4
SEQ = 8192
DEPTH = 2

N_META = 16
CONV_WIDTH = 4
NORM_EPS = 1e-6
RG_WIDTH = 2 * D_MODEL
RG_BLOCK = 256
RG_BLOCKS = RG_WIDTH // RG_BLOCK
RG_C = 8.0
DN_HEAD_DIM = 128
DN_QK_HEADS = D_MODEL // DN_HEAD_DIM
DN_V_HEADS = 2 * DN_QK_HEADS
DN_QK_WIDTH = DN_QK_HEADS * DN_HEAD_DIM
DN_V_WIDTH = DN_V_HEADS * DN_HEAD_DIM
DN_CONV_CH = 2 * DN_QK_WIDTH + DN_V_WIDTH
DN_PROJ = DN_CONV_CH + DN_V_WIDTH + 4 * DN_V_HEADS
CHUNK = 64
N_A_LAYERS = (DEPTH + 1) // 2
N_B_LAYERS = DEPTH // 2

kernel_name = "hybrid_rglru_gdn_meta_encoder"


def rms_norm(x, w):
    xf = x.astype(jnp.float32)
    y = xf * lax.rsqrt(jnp.mean(xf * xf, axis=-1, keepdims=True) + NORM_EPS)
    return (y * w.astype(jnp.float32)).astype(x.dtype)


def l2_norm(x):
    return x * lax.rsqrt(jnp.sum(x * x, axis=-1, keepdims=True) + NORM_EPS)


def dwconv(x, w):
    k, c = w.shape
    return lax.conv_general_dilated(
        x, w.astype(x.dtype)[:, None, :], window_strides=(1,),
        padding=[((k - 1) // 2, k // 2)],
        dimension_numbers=("NWC", "WIO", "NWC"), feature_group_count=c)


def linear_recurrence(a, b):
    def combine(l, r):
        return l[0] * r[0], r[0] * l[1] + r[1]
    _, h = lax.associative_scan(combine, (a, b), axis=1)
    return h


def rglru_mixer(y, w_in, conv_w, conv_b, gate_w, gate_b, lam, w_out):
    bsz, t, _ = y.shape
    xb, zb = jnp.split(y @ w_in, 2, axis=-1)
    xb = dwconv(xb, conv_w) + conv_b.astype(y.dtype)
    xf = xb.astype(jnp.float32)
    xh = xb.reshape(bsz, t, RG_BLOCKS, RG_BLOCK)
    outs = []
    for d in range(2):
        gates = jnp.einsum("btnc,gncm->gbtnm", xh, gate_w[d].astype(y.dtype)) \
            + gate_b[d][:, None, None].astype(y.dtype)
        r = jax.nn.sigmoid(gates[0].astype(jnp.float32)).reshape(bsz, t, RG_WIDTH)
        i = jax.nn.sigmoid(gates[1].astype(jnp.float32)).reshape(bsz, t, RG_WIDTH)
        log_a = -RG_C * jax.nn.softplus(-lam[d].astype(jnp.float32)) * r
        a = jnp.exp(log_a)
        b = jnp.sqrt(-jnp.expm1(2.0 * log_a)) * (i * xf)
        if d == 1:
            h = jnp.flip(linear_recurrence(jnp.flip(a, 1), jnp.flip(b, 1)), 1)
        else:
            h = linear_recurrence(a, b)
        outs.append(h)
    o = (outs[0] + outs[1]) * jax.nn.silu(zb.astype(jnp.float32))
    return o.astype(y.dtype) @ w_out


def chunk_gated_delta(q, k, v, g, beta):
    c = q.shape[3]
    dv = v.shape[-1]
    incl = jnp.tril(jnp.ones((c, c), dtype=bool))
    strict = jnp.tril(jnp.ones((c, c), dtype=bool), -1)
    gc = jnp.cumsum(g, axis=-1)
    diff = gc[..., :, None] - gc[..., None, :]
    decay = jnp.exp(jnp.where(incl, diff, -jnp.inf))
    kb = k * beta[..., None]
    a_mat = jnp.where(strict, jnp.einsum("bhnid,bhnjd->bhnij", kb, k) * decay, 0.0)
    eye = jnp.eye(c, dtype=q.dtype)
    rhs = jnp.concatenate([v * beta[..., None], kb * jnp.exp(gc)[..., None]], axis=-1)
    sol = lax.linalg.triangular_solve(a_mat + eye, rhs, left_side=True, lower=True,
                                      unit_diagonal=True)
    u, w = sol[..., :dv], sol[..., dv:]
    qk = jnp.einsum("bhnid,bhnjd->bhnij", q, k) * decay
    q_dec = q * jnp.exp(gc)[..., None]
    k_tail = k * jnp.exp(gc[..., -1:] - gc)[..., None]
    g_tot = jnp.exp(gc[..., -1])

    def step(s, inp):
        u_n, w_n, qk_n, qd_n, kt_n, gt_n = inp
        v_new = u_n - jnp.einsum("bhck,bhkv->bhcv", w_n, s)
        o = jnp.einsum("bhck,bhkv->bhcv", qd_n, s) + jnp.einsum("bhcj,bhjv->bhcv", qk_n, v_new)
        s = s * gt_n[..., None, None] + jnp.einsum("bhck,bhcv->bhkv", kt_n, v_new)
        return s, o

    bsz, hh = q.shape[0], q.shape[1]
    s0 = jnp.zeros((bsz, hh, q.shape[-1], dv), q.dtype)
    xs = tuple(jnp.moveaxis(z, 2, 0) for z in (u, w, qk, q_dec, k_tail, g_tot))
    _, o = lax.scan(step, s0, xs)
    return jnp.moveaxis(o, 0, 2)


def delta_direction(q, k, v, g, beta, reverse):
    bsz, t, hh, _ = q.shape
    n_pad = (-t) % CHUNK
    arrs = [q, k, v, g, beta]
    if reverse:
        arrs = [jnp.flip(z, 1) for z in arrs]
    lo, hi = (0, n_pad) if reverse else (n_pad, 0)
    arrs = [jnp.pad(z, [(0, 0), (lo, hi)] + [(0, 0)] * (z.ndim - 2)) for z in arrs]
    tp = t + n_pad
    nc = tp // CHUNK
    qc, kc, vc = [z.reshape(bsz, nc, CHUNK, hh, -1).transpose(0, 3, 1, 2, 4) for z in arrs[:3]]
    gcn, bc = [z.reshape(bsz, nc, CHUNK, hh).transpose(0, 3, 1, 2) for z in arrs[3:]]
    o = chunk_gated_delta(qc, kc, vc, gcn, bc)
    o = o.transpose(0, 2, 3, 1, 4).reshape(bsz, tp, hh, -1)[:, lo:tp - hi]
    return jnp.flip(o, 1) if reverse else o


def deltanet_mixer(y, w_in, conv_w, a_log, dt_bias, norm_w, w_out):
    bsz, t, _ = y.shape
    proj = y @ w_in
    qkv, z, ab = jnp.split(proj, [DN_CONV_CH, DN_CONV_CH + DN_V_WIDTH], axis=-1)
    qkv = jax.nn.silu(dwconv(qkv, conv_w))
    q, k, v = jnp.split(qkv, [DN_QK_WIDTH, 2 * DN_QK_WIDTH], axis=-1)
    rep = DN_V_HEADS // DN_QK_HEADS
    q = l2_norm(q.reshape(bsz, t, DN_QK_HEADS, DN_HEAD_DIM).astype(jnp.float32)) * (DN_HEAD_DIM ** -0.5)
    k = l2_norm(k.reshape(bsz, t, DN_QK_HEADS, DN_HEAD_DIM).astype(jnp.float32))
    q = jnp.repeat(q, rep, axis=2)
    k = jnp.repeat(k, rep, axis=2)
    v = v.reshape(bsz, t, DN_V_HEADS, DN_HEAD_DIM).astype(jnp.float32)
    ab = ab.astype(jnp.float32)
    a_in = ab[..., :2 * DN_V_HEADS].reshape(bsz, t, 2, DN_V_HEADS)
    b_in = ab[..., 2 * DN_V_HEADS:].reshape(bsz, t, 2, DN_V_HEADS)
    g = -jnp.exp(a_log.astype(jnp.float32)) * jax.nn.softplus(a_in + dt_bias.astype(jnp.float32))
    beta = jax.nn.sigmoid(b_in)
    o_f = delta_direction(q, k, v, g[:, :, 0], beta[:, :, 0], reverse=False)
    o_b = delta_direction(q, k, v, g[:, :, 1], beta[:, :, 1], reverse=True)
    o = rms_norm(o_f + o_b, norm_w) * jax.nn.silu(z.reshape(bsz, t, DN_V_HEADS, DN_HEAD_DIM).astype(jnp.float32))
    return o.reshape(bsz, t, DN_V_WIDTH).astype(y.dtype) @ w_out


def setup_inputs(seed: int = 0) -> dict:
    key = jax.random.key(seed)
    ks = jax.random.split(key, 16)
    nrm = jax.random.normal
    f32 = jnp.float32
    a0 = jax.random.uniform(ks[8], (N_A_LAYERS, 2, RG_WIDTH), f32, 0.9, 0.999)
    dt = jax.random.uniform(ks[13], (N_B_LAYERS, 2, DN_V_HEADS), f32, 0.001, 0.1)
    return {
        "x": nrm(ks[0], (BATCH, SEQ, D_MODEL), f32),
        "meta_tokens": nrm(ks[1], (N_META, D_MODEL), f32),
        "norm_w": 1.0 + 0.1 * nrm(ks[2], (DEPTH, 2, D_MODEL), f32),
        "rg_w_in": nrm(ks[3], (N_A_LAYERS, D_MODEL, 2 * RG_WIDTH), f32) * D_MODEL ** -0.5,
        "rg_conv_w": nrm(ks[4], (N_A_LAYERS, CONV_WIDTH, RG_WIDTH), f32) * CONV_WIDTH ** -0.5,
        "rg_conv_b": 0.01 * nrm(ks[5], (N_A_LAYERS, RG_WIDTH), f32),
        "rg_gate_w": nrm(ks[6], (N_A_LAYERS, 2, 2, RG_BLOCKS, RG_BLOCK, RG_BLOCK), f32) * RG_BLOCK ** -0.5,
        "rg_gate_b": 0.1 * nrm(ks[7], (N_A_LAYERS, 2, 2, RG_BLOCKS, RG_BLOCK), f32),
        "rg_lambda": jnp.log(a0) - jnp.log1p(-a0),
        "rg_w_out": nrm(ks[9], (N_A_LAYERS, RG_WIDTH, D_MODEL), f32) * RG_WIDTH ** -0.5,
        "dn_w_in": nrm(ks[10], (N_B_LAYERS, D_MODEL, DN_PROJ), f32) * D_MODEL ** -0.5,
        "dn_conv_w": nrm(ks[11], (N_B_LAYERS, CONV_WIDTH, DN_CONV_CH), f32) * CONV_WIDTH ** -0.5,
        "dn_a_log": jnp.log(jax.random.uniform(ks[12], (N_B_LAYERS, 2, DN_V_HEADS), f32, 1.0, 16.0)),
        "dn_dt_bias": jnp.log(jnp.expm1(dt)),
        "dn_norm_w": 1.0 + 0.1 * nrm(ks[14], (N_B_LAYERS, DN_HEAD_DIM), f32),
        "dn_w_out": nrm(ks[15], (N_B_LAYERS, DN_V_WIDTH, D_MODEL), f32) * DN_V_WIDTH ** -0.5,
    }


def reference(x, meta_tokens, norm_w, rg_w_in, rg_conv_w, rg_conv_b, rg_gate_w, rg_gate_b,
              rg_lambda, rg_w_out, dn_w_in, dn_conv_w, dn_a_log, dn_dt_bias, dn_norm_w, dn_w_out):
    bsz = x.shape[0]
    meta = jnp.broadcast_to(meta_tokens.astype(x.dtype)[None], (bsz, N_META, x.shape[-1]))
    h = jnp.concatenate([meta, x], axis=1)
    for i in range(DEPTH):
        y = rms_norm(h, norm_w[i, 0])
        j = i // 2
        if i % 2 == 0:
            y = rglru_mixer(y, rg_w_in[j], rg_conv_w[j], rg_conv_b[j], rg_gate_w[j],
                            rg_gate_b[j], rg_lambda[j], rg_w_out[j])
        else:
            y = deltanet_mixer(y, dn_w_in[j], dn_conv_w[j], dn_a_log[j], dn_dt_bias[j],
                               dn_norm_w[j], dn_w_out[j])
        h = h + rms_norm(y, norm_w[i, 1])
    return h[:, N_META:]
```

```python
import functools

import jax
import jax.numpy as jnp
from jax import lax
from jax.experimental import pallas as pl
from jax.experimental.pallas import tpu as pltpu

F32 = jnp.float32
BF16 = jnp.bfloat16

NORM_EPS = 1e-6
RG_C = 8.0
RG_BLOCK = 256
DN_HEAD_DIM = 128
CONV_WIDTH = 4

TIME_TILE = 256
HALO = 16
CHUNK = 128
LANES = 128
SCAN_SLAB_PAD = 8
VMEM_LIMIT = 56 * 1024 * 1024


def _nn(a, b):
    return jnp.dot(a, b, preferred_element_type=F32)


def _nt(a, b):
    return lax.dot_general(a, b, (((1,), (1,)), ((), ())), preferred_element_type=F32)


def _tn(a, b):
    return lax.dot_general(a, b, (((0,), (0,)), ((), ())), preferred_element_type=F32)


def _sigmoid(x):
    return 0.5 * jnp.tanh(0.5 * x) + 0.5


def _softplus(x):
    return jnp.maximum(x, 0.0) + jnp.log1p(jnp.exp(-jnp.abs(x)))


def _rms(h, gain):
    return h * lax.rsqrt(jnp.mean(h * h, axis=-1, keepdims=True) + NORM_EPS) * gain


def _split_hi_lo(x):
    hi = x.astype(BF16)
    lo = (x - hi.astype(F32)).astype(BF16)
    return hi, lo


def _fill_ybuf(hp_ref, hm_ref, hn_ref, gain, ybuf, i, nt, tt):
    ym = _rms(hm_ref[0], gain).astype(BF16)
    yp = jnp.where(i > 0, _rms(hp_ref[0], gain), 0.0).astype(BF16)
    yn = jnp.where(i < nt - 1, _rms(hn_ref[0], gain), 0.0).astype(BF16)
    ybuf[0:HALO, :] = yp
    ybuf[HALO:HALO + tt, :] = ym
    ybuf[HALO + tt:, :] = yn
    return ym


def _conv_cols(pre, cw_ref, c0, c1, tt):
    acc = None
    for k in range(CONV_WIDTH):
        term = cw_ref[k:k + 1, c0:c1] * pre[HALO - 1 + k:HALO - 1 + k + tt, c0:c1]
        acc = term if acc is None else acc + term
    return acc


def _rg_in_kernel(hp_ref, hm_ref, hn_ref, gain_ref, wx_ref, wz_ref, cw_ref, cb_ref,
                  xb_ref, zb_ref, ybuf, pre, *, tt, nt, npad):
    i = pl.program_id(1)
    ym = _fill_ybuf(hp_ref, hm_ref, hn_ref, gain_ref[...], ybuf, i, nt, tt)
    pre[...] = _nn(ybuf[...], wx_ref[...])
    row = i * tt + lax.broadcasted_iota(jnp.int32, (tt, 1), 0)
    valid = row >= npad
    width = xb_ref.shape[-1]
    for c0 in range(0, width, 512):
        c1 = c0 + 512
        acc = _conv_cols(pre, cw_ref, c0, c1, tt) + cb_ref[:, c0:c1]
        xb_ref[0, :, c0:c1] = jnp.where(valid, acc, 0.0)
    zb_ref[0] = _nn(ym, wz_ref[...])


def _rg_scan_kernel(xf_ref, xr_ref, gw_ref, gb_ref, lam_ref, hf_ref, hr_ref,
                    af_scr, bf_scr, ar_scr, br_scr, h_scr, *, tt, pitch):
    i = pl.program_id(1)

    @pl.when(i == 0)
    def _():
        h_scr[...] = jnp.zeros_like(h_scr)

    width = xf_ref.shape[-1]
    nblk = width // RG_BLOCK
    for d, (x_ref, a_scr, b_scr) in enumerate(((xf_ref, af_scr, bf_scr), (xr_ref, ar_scr, br_scr))):
        for n in range(nblk):
            c0 = n * RG_BLOCK
            x = x_ref[0, :, c0:c0 + RG_BLOCK]
            xb16 = x.astype(BF16)
            r = _sigmoid(_nn(xb16, gw_ref[d, 0, n]) + gb_ref[d, 0, :, c0:c0 + RG_BLOCK])
            g_in = _sigmoid(_nn(xb16, gw_ref[d, 1, n]) + gb_ref[d, 1, :, c0:c0 + RG_BLOCK])
            log_a = (-RG_C * _softplus(-lam_ref[d, :, c0:c0 + RG_BLOCK])) * r
            a = jnp.exp(log_a)
            mult = jnp.sqrt(-jnp.tanh(log_a) * (a * a + 1.0))
            b = mult * (g_in * x)
            for s in range(RG_BLOCK // LANES):
                q = n * (RG_BLOCK // LANES) + s
                a_scr[q * pitch:q * pitch + tt, :] = a[:, s * LANES:(s + 1) * LANES]
                b_scr[q * pitch:q * pitch + tt, :] = b[:, s * LANES:(s + 1) * LANES]

    nslab = width // LANES
    ngrp = nslab // 8

    def body(t, carry):
        tr = tt - 1 - t
        new = []
        for g in range(ngrp):
            base = g * 8 * pitch
            idx = pl.ds(base + t, 8, stride=pitch)
            h = af_scr[idx, :] * carry[g] + bf_scr[idx, :]
            bf_scr[idx, :] = h
            new.append(h)
        for g in range(ngrp):
            base = g * 8 * pitch
            idx = pl.ds(base + tr, 8, stride=pitch)
            h = ar_scr[idx, :] * carry[ngrp + g] + br_scr[idx, :]
            br_scr[idx, :] = h
            new.append(h)
        return tuple(new)

    init = tuple(h_scr[j] for j in range(2 * ngrp))
    final = lax.fori_loop(0, tt, body, init, unroll=8)
    for j in range(2 * ngrp):
        h_scr[j] = final[j]
    for q in range(nslab):
        hf_ref[0, :, q * LANES:(q + 1) * LANES] = bf_scr[q * pitch:q * pitch + tt, :]
        hr_ref[0, :, q * LANES:(q + 1) * LANES] = br_scr[q * pitch:q * pitch + tt, :]


def _rg_out_kernel(hf_ref, hr_ref, z_ref, h0_ref, wo_ref, gain_ref, out_ref):
    z = z_ref[0]
    o = (hf_ref[0] + hr_ref[0]) * (z * _sigmoid(z))
    y = _nn(o.astype(BF16), wo_ref[...])
    out_ref[0] = h0_ref[0] + _rms(y, gain_ref[...])


def _dn_in_kernel(hp_ref, hm_ref, hn_ref, gain_ref, wqkv_ref, wz_ref, wab_ref, wabt_ref, cw_ref,
                  alog_row_ref, dtb_row_ref, alog_col_ref, dtb_col_ref, lmat_ref, umat_ref,
                  q_ref, k_ref, v_ref, z_ref, gb_ref, gct_ref, ybuf, pre, *, tt, nt, npad, qk_width):
    i = pl.program_id(1)
    ym = _fill_ybuf(hp_ref, hm_ref, hn_ref, gain_ref[...], ybuf, i, nt, tt)
    pre[...] = _nn(ybuf[...], wqkv_ref[...])
    row = i * tt + lax.broadcasted_iota(jnp.int32, (tt, 1), 0)
    valid = row >= npad
    scale = DN_HEAD_DIM ** -0.5
    for hd in range(2 * qk_width // DN_HEAD_DIM):
        c0 = hd * DN_HEAD_DIM
        c1 = c0 + DN_HEAD_DIM
        acc = _conv_cols(pre, cw_ref, c0, c1, tt)
        acc = acc * _sigmoid(acc)
        nrm = acc * lax.rsqrt(jnp.sum(acc * acc, axis=-1, keepdims=True) + NORM_EPS)
        nrm = jnp.where(valid, nrm, 0.0)
        if c0 < qk_width:
            q_ref[0, :, c0:c1] = nrm * scale
        else:
            k_ref[0, :, c0 - qk_width:c1 - qk_width] = nrm
    v_width = v_ref.shape[-1]
    for c0 in range(0, v_width, 512):
        acc = _conv_cols(pre, cw_ref, 2 * qk_width + c0, 2 * qk_width + c0 + 512, tt)
        v_ref[0, :, c0:c0 + 512] = jnp.where(valid, acc * _sigmoid(acc), 0.0)
    z_ref[0] = _nn(ym, wz_ref[...])

    lane = lax.broadcasted_iota(jnp.int32, (1, LANES), 1)
    nheads = gct_ref.shape[1] // 2
    for d in range(2):
        ab = _nn(ym, wab_ref[d])
        g = -jnp.exp(alog_row_ref[d]) * _softplus(ab + dtb_row_ref[d])
        g = jnp.where(valid, g, 0.0)
        g_hi, g_lo = _split_hi_lo(g)
        cmat = lmat_ref[...] if d == 0 else umat_ref[...]
        gc = _nn(cmat, g_hi) + _nn(cmat, g_lo)
        beta = jnp.where(valid, _sigmoid(ab), 0.0)
        gb_ref[d, 0] = jnp.where(lane < nheads, gc, beta)

    col = i * tt + lax.broadcasted_iota(jnp.int32, (1, tt), 1)
    abt = _nt(wabt_ref[...], ym)
    gt = -jnp.exp(alog_col_ref[...]) * _softplus(abt + dtb_col_ref[...])
    gt = jnp.where(col >= npad, gt, 0.0)
    gt_hi, gt_lo = _split_hi_lo(gt)
    gct_f = _nn(gt_hi[:nheads], umat_ref[...]) + _nn(gt_lo[:nheads], umat_ref[...])
    gct_r = _nn(gt_hi[nheads:], lmat_ref[...]) + _nn(gt_lo[nheads:], lmat_ref[...])
    gct_ref[0, :nheads, :] = gct_f
    gct_ref[0, nheads:, :] = gct_r


def _delta_kernel(q_ref, k_ref, v_ref, gb_ref, gct_ref, o_ref, s_scr, *, nheads, rep):
    d = pl.program_id(1)
    c = pl.program_id(2)

    @pl.when(c == 0)
    def _():
        s_scr[...] = jnp.zeros_like(s_scr)

    cs = q_ref.shape[1]
    rows = lax.broadcasted_iota(jnp.int32, (cs, cs), 0)
    cols = lax.broadcasted_iota(jnp.int32, (cs, cs), 1)
    fwd = d == 0
    later = jnp.where(fwd, rows, cols)
    earlier = jnp.where(fwd, cols, rows)
    incl = later >= earlier
    strict = later > earlier
    eye = (rows == cols).astype(F32)

    gb = gb_ref[0, 0]
    gct = gct_ref[0]
    gtot_row = jnp.where(fwd, gb[cs - 1:cs, :], gb[0:1, :])
    gdiff = gtot_row - gb

    for p in range(nheads // rep):
        qp = q_ref[0, :, p * DN_HEAD_DIM:(p + 1) * DN_HEAD_DIM]
        kp = k_ref[0, :, p * DN_HEAD_DIM:(p + 1) * DN_HEAD_DIM]
        kp16 = kp.astype(BF16)
        kk = _nt(kp16, kp16)
        qk0 = _nt(qp.astype(BF16), kp16)
        for hh in range(rep):
            h = p * rep + hh
            gcol = gb[:, h:h + 1]
            bcol = gb[:, nheads + h:nheads + h + 1]
            grow = gct[h:h + 1, :]
            decay = jnp.exp(jnp.where(incl, gcol - grow, -1e30))
            a_mat = jnp.where(strict, kk * decay, 0.0) * bcol
            a16 = a_mat.astype(BF16)
            t_mat = eye - a_mat
            pw = _nn(a16, a16)
            span = 2
            while span < cs:
                pw16 = pw.astype(BF16)
                t_mat = t_mat + _nn(t_mat.astype(BF16), pw16)
                span *= 2
                if span < cs:
                    pw = _nn(pw16, pw16)
            egc = jnp.exp(gcol)
            vh = v_ref[0, :, h * DN_HEAD_DIM:(h + 1) * DN_HEAD_DIM]
            t16 = t_mat.astype(BF16)
            u = _nn(t16, (vh * bcol).astype(BF16))
            w = _nn(t16, (kp * (bcol * egc)).astype(BF16))
            qd = qp * egc
            kt = kp * jnp.exp(gdiff[:, h:h + 1])
            s = s_scr[h]
            s16 = s.astype(BF16)
            v_new = u - _nn(w.astype(BF16), s16)
            vn16 = v_new.astype(BF16)
            o = _nn(qd.astype(BF16), s16) + _nn((qk0 * decay).astype(BF16), vn16)
            gtot = jnp.exp(jnp.where(fwd, gcol[cs - 1:cs, :], gcol[0:1, :]))
            s_scr[h] = s * gtot + _tn(kt.astype(BF16), vn16)
            o_ref[0, 0, :, h * DN_HEAD_DIM:(h + 1) * DN_HEAD_DIM] = o


def _dn_out_kernel(of_ref, or_ref, z_ref, h1_ref, nw_ref, wo_ref, gain_ref, out_ref, ybuf):
    width = z_ref.shape[-1]
    for c0 in range(0, width, DN_HEAD_DIM):
        c1 = c0 + DN_HEAD_DIM
        o = of_ref[0, 0, :, c0:c1] + or_ref[0, 0, :, c0:c1]
        z = z_ref[0, :, c0:c1]
        y = o * lax.rsqrt(jnp.mean(o * o, axis=-1, keepdims=True) + NORM_EPS) * nw_ref[...]
        ybuf[:, c0:c1] = (y * (z * _sigmoid(z))).astype(BF16)
    y = _nn(ybuf[...], wo_ref[...])
    out_ref[0] = h1_ref[0] + _rms(y, gain_ref[...])


def _params(sem):
    return pltpu.CompilerParams(dimension_semantics=sem, vmem_limit_bytes=VMEM_LIMIT)


def _const_spec(shape):
    nd = len(shape)
    return pl.BlockSpec(shape, lambda *_: (0,) * nd)


def _halo_specs(tt, nt, d_model):
    per = tt // HALO
    last = nt * per - 1
    prev = pl.BlockSpec((1, HALO, d_model), lambda b, i: (b, jnp.maximum(i * per - 1, 0), 0))
    main = pl.BlockSpec((1, tt, d_model), lambda b, i: (b, i, 0))
    nxt = pl.BlockSpec((1, HALO, d_model), lambda b, i: (b, jnp.minimum((i + 1) * per, last), 0))
    return prev, main, nxt


def kernel(x, meta_tokens, norm_w, rg_w_in, rg_conv_w, rg_conv_b, rg_gate_w, rg_gate_b, rg_lambda,
           rg_w_out, dn_w_in, dn_conv_w, dn_a_log, dn_dt_bias, dn_norm_w, dn_w_out):
    bsz, seq, d_model = x.shape
    n_meta = meta_tokens.shape[0]
    tt = TIME_TILE
    assert seq % tt == 0 and n_meta <= tt and tt % CHUNK == 0
    npad = tt - n_meta
    tf = tt + seq
    nt = tf // tt
    rg_width = rg_w_out.shape[1]
    v_width = dn_w_out.shape[1]
    nheads = dn_a_log.shape[-1]
    qk_width = (dn_conv_w.shape[-1] - v_width) // 2
    rep = v_width // qk_width
    conv_ch = 2 * qk_width + v_width
    assert rg_width % (8 * LANES) == 0 and nheads * DN_HEAD_DIM == v_width and 2 * nheads <= LANES

    meta = jnp.broadcast_to(meta_tokens.astype(F32)[None], (bsz, n_meta, d_model))
    h0 = jnp.concatenate([jnp.zeros((bsz, npad, d_model), F32), meta, x], axis=1)

    prev_spec, main_spec, next_spec = _halo_specs(tt, nt, d_model)
    tile = lambda w: pl.BlockSpec((1, tt, w), lambda b, i: (b, i, 0))
    rtile = lambda w: pl.BlockSpec((1, tt, w), lambda b, i: (b, nt - 1 - i, 0))

    w_in = rg_w_in[0].astype(BF16)
    xb, zb = pl.pallas_call(
        functools.partial(_rg_in_kernel, tt=tt, nt=nt, npad=npad),
        grid=(bsz, nt),
        in_specs=[prev_spec, main_spec, next_spec, _const_spec((1, d_model)),
                  _const_spec((d_model, rg_width)), _const_spec((d_model, rg_width)),
                  _const_spec((CONV_WIDTH, rg_width)), _const_spec((1, rg_width))],
        out_specs=[tile(rg_width), tile(rg_width)],
        out_shape=[jax.ShapeDtypeStruct((bsz, tf, rg_width), F32)] * 2,
        scratch_shapes=[pltpu.VMEM((tt + 2 * HALO, d_model), BF16),
                        pltpu.VMEM((tt + 2 * HALO, rg_width), F32)],
        compiler_params=_params(("parallel", "arbitrary")),
        name="rg_in",
    )(h0, h0, h0, norm_w[0, 0][None], w_in[:, :rg_width], w_in[:, rg_width:],
      rg_conv_w[0], rg_conv_b[0][None])

    pitch = tt + SCAN_SLAB_PAD
    nslab = rg_width // LANES
    gate_w = rg_gate_w[0].astype(BF16)
    gate_b = rg_gate_b[0].reshape(2, 2, 1, rg_width)
    slab = pltpu.VMEM((nslab * pitch, LANES), F32)
    hf, hr = pl.pallas_call(
        functools.partial(_rg_scan_kernel, tt=tt, pitch=pitch),
        grid=(bsz, nt),
        in_specs=[tile(rg_width), rtile(rg_width), _const_spec(gate_w.shape),
                  _const_spec(gate_b.shape), _const_spec((2, 1, rg_width))],
        out_specs=[tile(rg_width), rtile(rg_width)],
        out_shape=[jax.ShapeDtypeStruct((bsz, tf, rg_width), F32)] * 2,
        scratch_shapes=[slab, slab, slab, slab, pltpu.VMEM((2 * nslab // 8, 8, LANES), F32)],
        compiler_params=_params(("parallel", "arbitrary")),
        name="rg_scan",
    )(xb, xb, gate_w, gate_b, rg_lambda[0][:, None, :])

    h1 = pl.pallas_call(
        _rg_out_kernel,
        grid=(bsz, nt),
        in_specs=[tile(rg_width), tile(rg_width), tile(rg_width), tile(d_model),
                  _const_spec((rg_width, d_model)), _const_spec((1, d_model))],
        out_specs=tile(d_model),
        out_shape=jax.ShapeDtypeStruct((bsz, tf, d_model), F32),
        compiler_params=_params(("parallel", "parallel")),
        name="rg_out",
    )(hf, hr, zb, h0, rg_w_out[0].astype(BF16), norm_w[0, 1][None])

    w_in = dn_w_in[0]
    w_ab = w_in[:, conv_ch + v_width:]
    w_a = w_ab[:, :2 * nheads].reshape(d_model, 2, nheads)
    w_b = w_ab[:, 2 * nheads:].reshape(d_model, 2, nheads)
    w_ab_nat = jnp.concatenate(
        [w_a, w_b, jnp.zeros((d_model, 2, LANES - 2 * nheads), F32)], axis=-1)
    w_ab_nat = jnp.moveaxis(w_ab_nat, 1, 0).astype(BF16)
    w_ab_t = w_ab[:, :2 * nheads].T.astype(BF16)
    lane_pad = lambda p: jnp.pad(p, ((0, 0), (0, LANES - nheads)))[:, None, :]
    ti = jnp.arange(tt)
    lmat = ((ti[:, None] >= ti[None, :]) & (ti[:, None] // CHUNK == ti[None, :] // CHUNK)).astype(BF16)
    umat = lmat.T

    q, k, v, z, gb, gct = pl.pallas_call(
        functools.partial(_dn_in_kernel, tt=tt, nt=nt, npad=npad, qk_width=qk_width),
        grid=(bsz, nt),
        in_specs=[prev_spec, main_spec, next_spec, _const_spec((1, d_model)),
                  _const_spec((d_model, conv_ch)), _const_spec((d_model, v_width)),
                  _const_spec((2, d_model, LANES)), _const_spec((2 * nheads, d_model)),
                  _const_spec((CONV_WIDTH, conv_ch)),
                  _const_spec((2, 1, LANES)), _const_spec((2, 1, LANES)),
                  _const_spec((2 * nheads, 1)), _const_spec((2 * nheads, 1)),
                  _const_spec((tt, tt)), _const_spec((tt, tt))],
        out_specs=[tile(qk_width), tile(qk_width), tile(v_width), tile(v_width),
                   pl.BlockSpec((2, 1, tt, LANES), lambda b, i: (0, b, i, 0)),
                   pl.BlockSpec((1, 2 * nheads, tt), lambda b, i: (b, 0, i))],
        out_shape=[jax.ShapeDtypeStruct((bsz, tf, qk_width), F32),
                   jax.ShapeDtypeStruct((bsz, tf, qk_width), F32),
                   jax.ShapeDtypeStruct((bsz, tf, v_width), F32),
                   jax.ShapeDtypeStruct((bsz, tf, v_width), F32),
                   jax.ShapeDtypeStruct((2, bsz, tf, LANES), F32),
                   jax.ShapeDtypeStruct((bsz, 2 * nheads, tf), F32)],
        scratch_shapes=[pltpu.VMEM((tt + 2 * HALO, d_model), BF16),
                        pltpu.VMEM((tt + 2 * HALO, conv_ch), F32)],
        compiler_params=_params(("parallel", "arbitrary")),
        name="dn_in",
    )(h1, h1, h1, norm_w[1, 0][None], w_in[:, :conv_ch].astype(BF16),
      w_in[:, conv_ch:conv_ch + v_width].astype(BF16), w_ab_nat, w_ab_t, dn_conv_w[0],
      lane_pad(dn_a_log[0]), lane_pad(dn_dt_bias[0]),
      dn_a_log[0].reshape(2 * nheads, 1), dn_dt_bias[0].reshape(2 * nheads, 1), lmat, umat)

    nch = tf // CHUNK
    cidx = lambda d, c: c + d * (nch - 1 - 2 * c)
    ctile = lambda w: pl.BlockSpec((1, CHUNK, w), lambda b, d, c: (b, cidx(d, c), 0))
    o2 = pl.pallas_call(
        functools.partial(_delta_kernel, nheads=nheads, rep=rep),
        grid=(bsz, 2, nch),
        in_specs=[ctile(qk_width), ctile(qk_width), ctile(v_width),
                  pl.BlockSpec((1, 1, CHUNK, LANES), lambda b, d, c: (d, b, cidx(d, c), 0)),
                  pl.BlockSpec((1, nheads, CHUNK), lambda b, d, c: (b, d, cidx(d, c)))],
        out_specs=pl.BlockSpec((1, 1, CHUNK, v_width), lambda b, d, c: (d, b, cidx(d, c), 0)),
        out_shape=jax.ShapeDtypeStruct((2, bsz, tf, v_width), F32),
        scratch_shapes=[pltpu.VMEM((nheads, DN_HEAD_DIM, DN_HEAD_DIM), F32)],
        compiler_params=_params(("parallel", "parallel", "arbitrary")),
        name="delta",
    )(q, k, v, gb, gct)

    xtile = lambda w: pl.BlockSpec((1, tt, w), lambda b, i: (b, i + 1, 0))
    out = pl.pallas_call(
        _dn_out_kernel,
        grid=(bsz, nt - 1),
        in_specs=[pl.BlockSpec((1, 1, tt, v_width), lambda b, i: (0, b, i + 1, 0)),
                  pl.BlockSpec((1, 1, tt, v_width), lambda b, i: (1, b, i + 1, 0)),
                  xtile(v_width), xtile(d_model), _const_spec((1, DN_HEAD_DIM)),
                  _const_spec((v_width, d_model)), _const_spec((1, d_model))],
        out_specs=pl.BlockSpec((1, tt, d_model), lambda b, i: (b, i, 0)),
        out_shape=jax.ShapeDtypeStruct((bsz, seq, d_model), F32),
        scratch_shapes=[pltpu.VMEM((tt, v_width), BF16)],
        compiler_params=_params(("parallel", "parallel")),
        name="dn_out",
    )(o2, o2, z, h1, dn_norm_w[0][None], dn_w_out[0].astype(BF16), norm_w[1, 1][None])
    return out
```

```python
import functools

import jax
import jax.numpy as jnp
from jax import lax
from jax.experimental import pallas as pl
from jax.experimental.pallas import tpu as pltpu

F32 = jnp.float32
BF16 = jnp.bfloat16

NORM_EPS = 1e-6
RG_C = 8.0
RG_BLOCK = 256
DN_HEAD_DIM = 128
CONV_WIDTH = 4

TIME_TILE = 256
HALO = 16
CHUNK = 128
LANES = 128
SCAN_SLAB_PAD = 4
PROJ_BLOCK = 512
CONV_ROWS = 64
PRE_PITCH = 2
SCAN_UNROLL = 16
F32_TINY = 1.1754944e-38
DELTA_HEAD_GROUP = 16
INV_BASE = 8
VMEM_LIMIT = 56 * 1024 * 1024


def _nn(a, b):
    return jnp.dot(a, b, preferred_element_type=F32)


def _nt(a, b):
    return lax.dot_general(a, b, (((1,), (1,)), ((), ())), preferred_element_type=F32)


def _tn(a, b):
    return lax.dot_general(a, b, (((0,), (0,)), ((), ())), preferred_element_type=F32)


def _pair(a, b):
    return jnp.concatenate([a, b], axis=1)


def _block_diag(a, b):
    za = jnp.zeros((a.shape[0], b.shape[1]), a.dtype)
    zb = jnp.zeros((b.shape[0], a.shape[1]), a.dtype)
    return jnp.concatenate([_pair(a, za), _pair(zb, b)], axis=0)


def _sigmoid(x):
    return 0.5 * jnp.tanh(0.5 * x) + 0.5


def _silu(x):
    h = 0.5 * x
    return h * jnp.tanh(h) + h


def _softplus(x):
    return jnp.maximum(x, 0.0) + jnp.log1p(jnp.exp(-jnp.abs(x)))


def _rms(h, gain):
    return h * lax.rsqrt(jnp.mean(h * h, axis=-1, keepdims=True) + NORM_EPS) * gain


def _split_hi_lo(x):
    hi = x.astype(BF16)
    lo = (x - hi.astype(F32)).astype(BF16)
    return hi, lo


def _fill_ybuf(hp_ref, hm_ref, hn_ref, gain, ybuf, i, nt, tt):
    ym = _rms(hm_ref[0], gain).astype(BF16)
    yp = jnp.where(i > 0, _rms(hp_ref[0], gain), 0.0).astype(BF16)
    yn = jnp.where(i < nt - 1, _rms(hn_ref[0], gain), 0.0).astype(BF16)
    ybuf[0:HALO, :] = yp
    ybuf[HALO:HALO + tt, :] = ym
    ybuf[HALO + tt:, :] = yn
    return ym


def _store_pre(pre, res, c0):
    rows = res.shape[0]
    for s in range(res.shape[1] // LANES):
        pre[c0 // LANES + s, pl.ds(0, rows, stride=PRE_PITCH), :] = res[:, s * LANES:(s + 1) * LANES]


def _conv_tile(pre, cw_ref, r0, nrows, slab):
    acc = None
    for k in range(CONV_WIDTH):
        start = PRE_PITCH * (HALO - 1 + k + r0)
        tap = pre[slab, pl.ds(start, nrows, stride=PRE_PITCH), :]
        term = cw_ref[k:k + 1, slab * LANES:(slab + 1) * LANES] * tap
        acc = term if acc is None else acc + term
    return acc


def _rg_in_kernel(hp_ref, hm_ref, hn_ref, gain_ref, wx_ref, wz_ref, cw_ref, cb_ref,
                  xb_ref, zb_ref, ybuf, pre, *, tt, nt, npad):
    i = pl.program_id(1)
    ym = _fill_ybuf(hp_ref, hm_ref, hn_ref, gain_ref[...], ybuf, i, nt, tt)
    width = xb_ref.shape[-1]
    nblk = width // PROJ_BLOCK
    zblk = zb_ref.shape[-1] // nblk
    for j in range(nblk + 1):
        if j < nblk:
            c0 = j * PROJ_BLOCK
            _store_pre(pre, _nn(ybuf[...], wx_ref[:, c0:c0 + PROJ_BLOCK]), c0)
            zb_ref[0, :, j * zblk:(j + 1) * zblk] = _nn(ym, wz_ref[:, j * zblk:(j + 1) * zblk])
        if j > 0:
            for slab in range((j - 1) * PROJ_BLOCK // LANES, j * PROJ_BLOCK // LANES):
                lanes = slice(slab * LANES, (slab + 1) * LANES)
                for r0 in range(0, tt, CONV_ROWS):
                    xb_ref[0, r0:r0 + CONV_ROWS, lanes] = (
                        _conv_tile(pre, cw_ref, r0, CONV_ROWS, slab) + cb_ref[:, lanes])

    @pl.when(i == 0)
    def _():
        xb_ref[0, :npad, :] = jnp.zeros((npad, width), F32)


def _rg_scan_kernel(xf_ref, xr_ref, gw_ref, gb_ref, lam_ref, hf_ref, hr_ref,
                    af_scr, bf_scr, ar_scr, br_scr, of_scr, or_scr, h_scr, *, tt, pitch):
    i = pl.program_id(1)

    @pl.when(i == 0)
    def _():
        h_scr[...] = jnp.zeros_like(h_scr)

    width = xf_ref.shape[-1]
    nblk = width // RG_BLOCK
    for d, (x_ref, a_scr, b_scr) in enumerate(((xf_ref, af_scr, bf_scr), (xr_ref, ar_scr, br_scr))):
        for n in range(nblk):
            c0 = n * RG_BLOCK
            x = x_ref[0, :, c0:c0 + RG_BLOCK]
            xb16 = x.astype(BF16)
            tr = jnp.tanh(_nn(xb16, gw_ref[d, 0, n]) + gb_ref[d, 0, :, c0:c0 + RG_BLOCK])
            ti = jnp.tanh(_nn(xb16, gw_ref[d, 1, n]) + gb_ref[d, 1, :, c0:c0 + RG_BLOCK])
            kk = (0.5 * RG_C) * _softplus(-lam_ref[d, :, c0:c0 + RG_BLOCK])
            neg_log_a = kk * tr + kk
            a = jnp.exp(-neg_log_a)
            y = jnp.tanh(neg_log_a) * (a * a + 1.0)
            mult = y * lax.rsqrt(jnp.maximum(y, F32_TINY))
            b = (mult * x) * (0.5 * ti + 0.5)
            for s in range(RG_BLOCK // LANES):
                q = n * (RG_BLOCK // LANES) + s
                a_scr[q * pitch:q * pitch + tt, :] = a[:, s * LANES:(s + 1) * LANES]
                b_scr[q * pitch:q * pitch + tt, :] = b[:, s * LANES:(s + 1) * LANES]

    nslab = width // LANES
    ngrp = nslab // 8

    def body(t, carry):
        tr = tt - 1 - t
        new = []
        for g in range(ngrp):
            idx = pl.ds(g * 8 * pitch + t, 8, stride=pitch)
            h = af_scr[idx, :] * carry[g] + bf_scr[idx, :]
            of_scr[idx, :] = h
            new.append(h)
        for g in range(ngrp):
            idx = pl.ds(g * 8 * pitch + tr, 8, stride=pitch)
            h = ar_scr[idx, :] * carry[ngrp + g] + br_scr[idx, :]
            or_scr[idx, :] = h
            new.append(h)
        return tuple(new)

    init = tuple(h_scr[j] for j in range(2 * ngrp))
    final = lax.fori_loop(0, tt, body, init, unroll=SCAN_UNROLL)
    for j in range(2 * ngrp):
        h_scr[j] = final[j]
    for q in range(nslab):
        hf_ref[0, :, q * LANES:(q + 1) * LANES] = of_scr[q * pitch:q * pitch + tt, :]
        hr_ref[0, :, q * LANES:(q + 1) * LANES] = or_scr[q * pitch:q * pitch + tt, :]


def _rg_out_kernel(hf_ref, hr_ref, z_ref, h0_ref, wo_ref, gain_ref, out_ref):
    z = z_ref[0]
    o = (hf_ref[0] + hr_ref[0]) * _silu(z)
    y = _nn(o.astype(BF16), wo_ref[...])
    out_ref[0] = h0_ref[0] + _rms(y, gain_ref[...])


def _dn_in_kernel(hp_ref, hm_ref, hn_ref, gain_ref, wqkv_ref, wz_ref, wab_ref, wabt_ref, cw_ref,
                  alog_row_ref, dtb_row_ref, alog_col_ref, dtb_col_ref, lmat_ref, umat_ref,
                  q_ref, k_ref, v_ref, z_ref, gb_ref, gct_ref, ybuf, pre, *, tt, nt, npad, qk_width):
    i = pl.program_id(1)
    ym = _fill_ybuf(hp_ref, hm_ref, hn_ref, gain_ref[...], ybuf, i, nt, tt)
    row = i * tt + lax.broadcasted_iota(jnp.int32, (tt, 1), 0)
    valid = row >= npad
    scale = DN_HEAD_DIM ** -0.5
    v_width = v_ref.shape[-1]
    nblk = (2 * qk_width + v_width) // PROJ_BLOCK
    zblk = z_ref.shape[-1] // nblk

    def consume(c0):
        if c0 < 2 * qk_width:
            for h0 in range(c0, c0 + PROJ_BLOCK, DN_HEAD_DIM):
                for r0 in range(0, tt, CONV_ROWS):
                    acc = _silu(_conv_tile(pre, cw_ref, r0, CONV_ROWS, h0 // LANES))
                    inv = lax.rsqrt(jnp.sum(acc * acc, axis=-1, keepdims=True) + NORM_EPS)
                    if h0 < qk_width:
                        q_ref[0, r0:r0 + CONV_ROWS, h0:h0 + DN_HEAD_DIM] = acc * (inv * scale)
                    else:
                        k0 = h0 - qk_width
                        k_ref[0, r0:r0 + CONV_ROWS, k0:k0 + DN_HEAD_DIM] = acc * inv
        else:
            for slab in range(c0 // LANES, (c0 + PROJ_BLOCK) // LANES):
                v0 = slab * LANES - 2 * qk_width
                for r0 in range(0, tt, CONV_ROWS):
                    v_ref[0, r0:r0 + CONV_ROWS, v0:v0 + LANES] = _silu(_conv_tile(pre, cw_ref, r0, CONV_ROWS, slab))

    for j in range(nblk + 1):
        if j < nblk:
            c0 = j * PROJ_BLOCK
            _store_pre(pre, _nn(ybuf[...], wqkv_ref[:, c0:c0 + PROJ_BLOCK]), c0)
            z_ref[0, :, j * zblk:(j + 1) * zblk] = _nn(ym, wz_ref[:, j * zblk:(j + 1) * zblk])
        if j > 0:
            consume((j - 1) * PROJ_BLOCK)

    @pl.when(i == 0)
    def _():
        q_ref[0, :npad, :] = jnp.zeros((npad, qk_width), F32)
        k_ref[0, :npad, :] = jnp.zeros((npad, qk_width), F32)
        v_ref[0, :npad, :] = jnp.zeros((npad, v_width), F32)

    lane = lax.broadcasted_iota(jnp.int32, (1, LANES), 1)
    nheads = gct_ref.shape[1] // 2
    for d in range(2):
        ab = _nn(ym, wab_ref[d])
        g = -jnp.exp(alog_row_ref[d]) * _softplus(ab + dtb_row_ref[d])
        g = jnp.where(valid, g, 0.0)
        g_hi, g_lo = _split_hi_lo(g)
        cmat = lmat_ref[...] if d == 0 else umat_ref[...]
        gc = _nn(cmat, g_hi) + _nn(cmat, g_lo)
        beta = jnp.where(valid, _sigmoid(ab), 0.0)
        gb_ref[d, 0] = jnp.where(lane < nheads, gc, beta)

    col = i * tt + lax.broadcasted_iota(jnp.int32, (1, tt), 1)
    abt = _nt(wabt_ref[...], ym)
    gt = -jnp.exp(alog_col_ref[...]) * _softplus(abt + dtb_col_ref[...])
    gt = jnp.where(col >= npad, gt, 0.0)
    gt_hi, gt_lo = _split_hi_lo(gt)
    gct_f = _nn(gt_hi[:nheads], umat_ref[...]) + _nn(gt_lo[:nheads], umat_ref[...])
    gct_r = _nn(gt_hi[nheads:], lmat_ref[...]) + _nn(gt_lo[nheads:], lmat_ref[...])
    gct_ref[0, :nheads, :] = gct_f
    gct_ref[0, nheads:, :] = gct_r


def _delta_kernel(q_ref, k_ref, v_ref, gb_ref, gct_ref, o_ref, s_scr, *, nheads, rep):
    d = pl.program_id(1)
    c = pl.program_id(2)

    @pl.when(c == 0)
    def _():
        s_scr[...] = jnp.zeros_like(s_scr)

    cs = q_ref.shape[1]
    rows = lax.broadcasted_iota(jnp.int32, (cs, cs), 0)
    cols = lax.broadcasted_iota(jnp.int32, (cs, cs), 1)
    fwd = d == 0
    later = jnp.where(fwd, rows, cols)
    earlier = jnp.where(fwd, cols, rows)
    incl = later >= earlier
    strict = later > earlier
    block_xor = rows ^ cols
    base_mask = block_xor < INV_BASE
    band_masks = []
    size = INV_BASE
    while size < cs:
        band_masks.append((block_xor >> (size.bit_length() - 1)) == 1)
        size *= 2

    gb = gb_ref[0, 0]
    gct = gct_ref[0]
    gtot_row = jnp.where(fwd, gb[cs - 1:cs, :], gb[0:1, :])
    gdiff = gtot_row - gb

    hd = DN_HEAD_DIM
    for g0 in range(0, nheads, DELTA_HEAD_GROUP):
        heads = list(range(g0, g0 + DELTA_HEAD_GROUP))
        pairs = list(range(g0 // 2, (g0 + DELTA_HEAD_GROUP) // 2))
        qp = {p: q_ref[0, :, p * hd:(p + 1) * hd] for p in pairs}
        kp = {p: k_ref[0, :, p * hd:(p + 1) * hd] for p in pairs}
        kp16 = {p: kp[p].astype(BF16) for p in pairs}
        kk = {p: _nt(kp16[p], kp16[p]) for p in pairs}
        qk0 = {p: _nt(qp[p].astype(BF16), kp16[p]) for p in pairs}
        gcol = {h: gb[:, h:h + 1] for h in heads}
        bcol = {h: gb[:, nheads + h:nheads + h + 1] for h in heads}
        decay = {h: jnp.exp(jnp.where(incl, gcol[h] - gct[h:h + 1, :], -1e30)) for h in heads}
        a_mat = {h: jnp.where(strict, kk[h // 2] * decay[h], 0.0) * bcol[h] for h in heads}
        qkd16 = {p: _pair(*[(qk0[p] * decay[h]).astype(BF16) for h in (2 * p, 2 * p + 1)]) for p in pairs}
        dm = {h: -jnp.where(base_mask, a_mat[h], 0.0) for h in heads}
        qk = dm
        span = 1
        while span < INV_BASE:
            qk16 = {h: qk[h].astype(BF16) for h in heads}
            span *= 2
            if span == 2:
                qk = {h: _nn(qk16[h], qk16[h]) for h in heads}
                continue
            if span < INV_BASE:
                both = {h: _nn(qk16[h], _pair(qk16[h], dm[h].astype(BF16))) for h in heads}
                dm = {h: dm[h] + qk[h] + both[h][:, cs:] for h in heads}
                qk = {h: both[h][:, :cs] for h in heads}
            else:
                dm = {h: dm[h] + qk[h] + _nn(qk16[h], dm[h].astype(BF16)) for h in heads}
        for band in band_masks:
            ao = {h: jnp.where(band, a_mat[h], 0.0) for h in heads}
            ya = {h: ao[h] + _nn(dm[h].astype(BF16), ao[h].astype(BF16)) for h in heads}
            dm = {h: dm[h] - ya[h] - _nn(ya[h].astype(BF16), dm[h].astype(BF16)) for h in heads}
        egc = {h: jnp.exp(gcol[h]) for h in heads}
        uw = {}
        for h in heads:
            rhs = _pair(v_ref[0, :, h * hd:(h + 1) * hd] * bcol[h], kp[h // 2] * (bcol[h] * egc[h]))
            uw[h] = rhs + _nn(dm[h].astype(BF16), rhs.astype(BF16))
        s = {h: s_scr[h] for h in heads}
        s_bd = {p: _block_diag(s[2 * p].astype(BF16), s[2 * p + 1].astype(BF16)) for p in pairs}
        wq = {}
        for p in pairs:
            w_cat = _pair(uw[2 * p][:, hd:], uw[2 * p + 1][:, hd:]).astype(BF16)
            qd_cat = _pair(qp[p] * egc[2 * p], qp[p] * egc[2 * p + 1]).astype(BF16)
            wq[p] = _nn(jnp.concatenate([w_cat, qd_cat], axis=0), s_bd[p])
        vn16 = {p: (_pair(uw[2 * p][:, :hd], uw[2 * p + 1][:, :hd]) - wq[p][:cs]).astype(BF16) for p in pairs}
        for p in pairs:
            vn_bd = _block_diag(vn16[p][:, :hd], vn16[p][:, hd:])
            o_ref[0, 0, :, 2 * p * hd:(2 * p + 2) * hd] = wq[p][cs:] + _nn(qkd16[p], vn_bd)
        for h in heads:
            kt = kp[h // 2] * jnp.exp(gdiff[:, h:h + 1])
            gtot = jnp.exp(jnp.where(fwd, gcol[h][cs - 1:cs, :], gcol[h][0:1, :]))
            vn_h = vn16[h // 2][:, (h % 2) * hd:(h % 2 + 1) * hd]
            s_scr[h] = s[h] * gtot + _tn(kt.astype(BF16), vn_h)


def _dn_out_kernel(of_ref, or_ref, z_ref, h1_ref, nw_ref, wo_ref, gain_ref, out_ref, ybuf):
    width = z_ref.shape[-1]
    for c0 in range(0, width, DN_HEAD_DIM):
        c1 = c0 + DN_HEAD_DIM
        o = of_ref[0, 0, :, c0:c1] + or_ref[0, 0, :, c0:c1]
        z = z_ref[0, :, c0:c1]
        y = o * lax.rsqrt(jnp.mean(o * o, axis=-1, keepdims=True) + NORM_EPS) * nw_ref[...]
        ybuf[:, c0:c1] = (y * _silu(z)).astype(BF16)
    y = _nn(ybuf[...], wo_ref[...])
    out_ref[0] = h1_ref[0] + _rms(y, gain_ref[...])


def _params(sem):
    return pltpu.CompilerParams(dimension_semantics=sem, vmem_limit_bytes=VMEM_LIMIT)


def _const_spec(shape):
    nd = len(shape)
    return pl.BlockSpec(shape, lambda *_: (0,) * nd)


def _halo_specs(tt, nt, d_model):
    per = tt // HALO
    last = nt * per - 1
    prev = pl.BlockSpec((1, HALO, d_model), lambda b, i: (b, jnp.maximum(i * per - 1, 0), 0))
    main = pl.BlockSpec((1, tt, d_model), lambda b, i: (b, i, 0))
    nxt = pl.BlockSpec((1, HALO, d_model), lambda b, i: (b, jnp.minimum((i + 1) * per, last), 0))
    return prev, main, nxt


def kernel(x, meta_tokens, norm_w, rg_w_in, rg_conv_w, rg_conv_b, rg_gate_w, rg_gate_b, rg_lambda,
           rg_w_out, dn_w_in, dn_conv_w, dn_a_log, dn_dt_bias, dn_norm_w, dn_w_out):
    bsz, seq, d_model = x.shape
    n_meta = meta_tokens.shape[0]
    tt = TIME_TILE
    assert seq % tt == 0 and n_meta <= tt and tt % CHUNK == 0
    npad = tt - n_meta
    tf = tt + seq
    nt = tf // tt
    rg_width = rg_w_out.shape[1]
    v_width = dn_w_out.shape[1]
    nheads = dn_a_log.shape[-1]
    qk_width = (dn_conv_w.shape[-1] - v_width) // 2
    rep = v_width // qk_width
    conv_ch = 2 * qk_width + v_width
    assert rg_width % (8 * LANES) == 0 and nheads * DN_HEAD_DIM == v_width and 2 * nheads <= LANES

    meta = jnp.broadcast_to(meta_tokens.astype(F32)[None], (bsz, n_meta, d_model))
    h0 = jnp.concatenate([jnp.zeros((bsz, npad, d_model), F32), meta, x], axis=1)

    prev_spec, main_spec, next_spec = _halo_specs(tt, nt, d_model)
    tile = lambda w: pl.BlockSpec((1, tt, w), lambda b, i: (b, i, 0))
    rtile = lambda w: pl.BlockSpec((1, tt, w), lambda b, i: (b, nt - 1 - i, 0))

    w_in = rg_w_in[0].astype(BF16)
    xb, zb = pl.pallas_call(
        functools.partial(_rg_in_kernel, tt=tt, nt=nt, npad=npad),
        grid=(bsz, nt),
        in_specs=[prev_spec, main_spec, next_spec, _const_spec((1, d_model)),
                  _const_spec((d_model, rg_width)), _const_spec((d_model, rg_width)),
                  _const_spec((CONV_WIDTH, rg_width)), _const_spec((1, rg_width))],
        out_specs=[tile(rg_width), tile(rg_width)],
        out_shape=[jax.ShapeDtypeStruct((bsz, tf, rg_width), F32)] * 2,
        scratch_shapes=[pltpu.VMEM((tt + 2 * HALO, d_model), BF16),
                        pltpu.VMEM((rg_width // LANES, PRE_PITCH * (tt + 2 * HALO), LANES), F32)],
        compiler_params=_params(("parallel", "arbitrary")),
        name="rg_in",
    )(h0, h0, h0, norm_w[0, 0][None], w_in[:, :rg_width], w_in[:, rg_width:],
      rg_conv_w[0], rg_conv_b[0][None])

    pitch = tt + SCAN_SLAB_PAD
    nslab = rg_width // LANES
    gate_w = (0.5 * rg_gate_w[0]).astype(BF16)
    gate_b = 0.5 * rg_gate_b[0].reshape(2, 2, 1, rg_width)
    slab = pltpu.VMEM((nslab * pitch, LANES), F32)
    hf, hr = pl.pallas_call(
        functools.partial(_rg_scan_kernel, tt=tt, pitch=pitch),
        grid=(bsz, nt),
        in_specs=[tile(rg_width), rtile(rg_width), _const_spec(gate_w.shape),
                  _const_spec(gate_b.shape), _const_spec((2, 1, rg_width))],
        out_specs=[tile(rg_width), rtile(rg_width)],
        out_shape=[jax.ShapeDtypeStruct((bsz, tf, rg_width), F32)] * 2,
        scratch_shapes=[slab] * 6 + [pltpu.VMEM((2 * nslab // 8, 8, LANES), F32)],
        compiler_params=_params(("parallel", "arbitrary")),
        name="rg_scan",
    )(xb, xb, gate_w, gate_b, rg_lambda[0][:, None, :])

    h1 = pl.pallas_call(
        _rg_out_kernel,
        grid=(bsz, nt),
        in_specs=[tile(rg_width), tile(rg_width), tile(rg_width), tile(d_model),
                  _const_spec((rg_width, d_model)), _const_spec((1, d_model))],
        out_specs=tile(d_model),
        out_shape=jax.ShapeDtypeStruct((bsz, tf, d_model), F32),
        compiler_params=_params(("parallel", "parallel")),
        name="rg_out",
    )(hf, hr, zb, h0, rg_w_out[0].astype(BF16), norm_w[0, 1][None])

    w_in = dn_w_in[0]
    w_ab = w_in[:, conv_ch + v_width:]
    w_a = w_ab[:, :2 * nheads].reshape(d_model, 2, nheads)
    w_b = w_ab[:, 2 * nheads:].reshape(d_model, 2, nheads)
    w_ab_nat = jnp.concatenate(
        [w_a, w_b, jnp.zeros((d_model, 2, LANES - 2 * nheads), F32)], axis=-1)
    w_ab_nat = jnp.moveaxis(w_ab_nat, 1, 0).astype(BF16)
    w_ab_t = w_ab[:, :2 * nheads].T.astype(BF16)
    lane_pad = lambda p: jnp.pad(p, ((0, 0), (0, LANES - nheads)))[:, None, :]
    ti = jnp.arange(tt)
    lmat = ((ti[:, None] >= ti[None, :]) & (ti[:, None] // CHUNK == ti[None, :] // CHUNK)).astype(BF16)
    umat = lmat.T

    q, k, v, z, gb, gct = pl.pallas_call(
        functools.partial(_dn_in_kernel, tt=tt, nt=nt, npad=npad, qk_width=qk_width),
        grid=(bsz, nt),
        in_specs=[prev_spec, main_spec, next_spec, _const_spec((1, d_model)),
                  _const_spec((d_model, conv_ch)), _const_spec((d_model, v_width)),
                  _const_spec((2, d_model, LANES)), _const_spec((2 * nheads, d_model)),
                  _const_spec((CONV_WIDTH, conv_ch)),
                  _const_spec((2, 1, LANES)), _const_spec((2, 1, LANES)),
                  _const_spec((2 * nheads, 1)), _const_spec((2 * nheads, 1)),
                  _const_spec((tt, tt)), _const_spec((tt, tt))],
        out_specs=[tile(qk_width), tile(qk_width), tile(v_width), tile(v_width),
                   pl.BlockSpec((2, 1, tt, LANES), lambda b, i: (0, b, i, 0)),
                   pl.BlockSpec((1, 2 * nheads, tt), lambda b, i: (b, 0, i))],
        out_shape=[jax.ShapeDtypeStruct((bsz, tf, qk_width), F32),
                   jax.ShapeDtypeStruct((bsz, tf, qk_width), F32),
                   jax.ShapeDtypeStruct((bsz, tf, v_width), F32),
                   jax.ShapeDtypeStruct((bsz, tf, v_width), F32),
                   jax.ShapeDtypeStruct((2, bsz, tf, LANES), F32),
                   jax.ShapeDtypeStruct((bsz, 2 * nheads, tf), F32)],
        scratch_shapes=[pltpu.VMEM((tt + 2 * HALO, d_model), BF16),
                        pltpu.VMEM((conv_ch // LANES, PRE_PITCH * (tt + 2 * HALO), LANES), F32)],
        compiler_params=_params(("parallel", "arbitrary")),
        name="dn_in",
    )(h1, h1, h1, norm_w[1, 0][None], w_in[:, :conv_ch].astype(BF16),
      w_in[:, conv_ch:conv_ch + v_width].astype(BF16), w_ab_nat, w_ab_t, dn_conv_w[0],
      lane_pad(dn_a_log[0]), lane_pad(dn_dt_bias[0]),
      dn_a_log[0].reshape(2 * nheads, 1), dn_dt_bias[0].reshape(2 * nheads, 1), lmat, umat)

    nch = tf // CHUNK
    cidx = lambda d, c: c + d * (nch - 1 - 2 * c)
    ctile = lambda w: pl.BlockSpec((1, CHUNK, w), lambda b, d, c: (b, cidx(d, c), 0))
    o2 = pl.pallas_call(
        functools.partial(_delta_kernel, nheads=nheads, rep=rep),
        grid=(bsz, 2, nch),
        in_specs=[ctile(qk_width), ctile(qk_width), ctile(v_width),
                  pl.BlockSpec((1, 1, CHUNK, LANES), lambda b, d, c: (d, b, cidx(d, c), 0)),
                  pl.BlockSpec((1, nheads, CHUNK), lambda b, d, c: (b, d, cidx(d, c)))],
        out_specs=pl.BlockSpec((1, 1, CHUNK, v_width), lambda b, d, c: (d, b, cidx(d, c), 0)),
        out_shape=jax.ShapeDtypeStruct((2, bsz, tf, v_width), F32),
        scratch_shapes=[pltpu.VMEM((nheads, DN_HEAD_DIM, DN_HEAD_DIM), F32)],
        compiler_params=_params(("parallel", "parallel", "arbitrary")),
        name="delta",
    )(q, k, v, gb, gct)

    xtile = lambda w: pl.BlockSpec((1, tt, w), lambda b, i: (b, i + 1, 0))
    out = pl.pallas_call(
        _dn_out_kernel,
        grid=(bsz, nt - 1),
        in_specs=[pl.BlockSpec((1, 1, tt, v_width), lambda b, i: (0, b, i + 1, 0)),
                  pl.BlockSpec((1, 1, tt, v_width), lambda b, i: (1, b, i + 1, 0)),
                  xtile(v_width), xtile(d_model), _const_spec((1, DN_HEAD_DIM)),
                  _const_spec((v_width, d_model)), _const_spec((1, d_model))],
        out_specs=pl.BlockSpec((1, tt, d_model), lambda b, i: (b, i, 0)),
        out_shape=jax.ShapeDtypeStruct((bsz, seq, d_model), F32),
        scratch_shapes=[pltpu.VMEM((tt, v_width), BF16)],
        compiler_params=_params(("parallel", "parallel")),
        name="dn_out",
    )(o2, o2, z, h1, dn_norm_w[0][None], dn_w_out[0].astype(BF16), norm_w[1, 1][None])
    return out
```

```python
import functools

import jax
import jax.numpy as jnp
from jax import lax
from jax.experimental import pallas as pl
from jax.experimental.pallas import tpu as pltpu

F32 = jnp.float32
BF16 = jnp.bfloat16
ACT = jnp.bfloat16

NORM_EPS = 1e-6
RG_C = 8.0
RG_BLOCK = 256
DN_HEAD_DIM = 128
CONV_WIDTH = 4

TIME_TILE = 256
HALO = 16
CHUNK = 128
LANES = 128
SCAN_SLAB_PAD = 4
RG_PROJ_BLOCK = 256
DN_PROJ_BLOCK = 1024
CONV_ROWS = 64
PRE_PITCH = 2
SCAN_UNROLL = 16
F32_TINY = 1.1754944e-38
DELTA_HEAD_GROUP = 16
INV_BASE = 8
VMEM_LIMIT = 56 * 1024 * 1024


def _nn(a, b):
    return jnp.dot(a, b, preferred_element_type=F32)


def _nt(a, b):
    return lax.dot_general(a, b, (((1,), (1,)), ((), ())), preferred_element_type=F32)


def _tn(a, b):
    return lax.dot_general(a, b, (((0,), (0,)), ((), ())), preferred_element_type=F32)


def _pair(a, b):
    return jnp.concatenate([a, b], axis=1)


def _block_diag(a, b):
    za = jnp.zeros((a.shape[0], b.shape[1]), a.dtype)
    zb = jnp.zeros((b.shape[0], a.shape[1]), a.dtype)
    return jnp.concatenate([_pair(a, za), _pair(zb, b)], axis=0)


def _sigmoid(x):
    return 0.5 * jnp.tanh(0.5 * x) + 0.5


def _silu(x):
    h = 0.5 * x
    return h * jnp.tanh(h) + h


def _softplus(x):
    return jnp.maximum(x, 0.0) + jnp.log1p(jnp.exp(-jnp.abs(x)))


def _rms(h, gain):
    return h * lax.rsqrt(jnp.mean(h * h, axis=-1, keepdims=True) + NORM_EPS) * gain


def _split_hi_lo(x):
    hi = x.astype(BF16)
    lo = (x - hi.astype(F32)).astype(BF16)
    return hi, lo


def _fill_ybuf(hp, hm, hn, gain, ybuf, i, nt, tt):
    ym = _rms(hm, gain).astype(BF16)
    yp = jnp.where(i > 0, _rms(hp, gain), 0.0).astype(BF16)
    yn = jnp.where(i < nt - 1, _rms(hn, gain), 0.0).astype(BF16)
    ybuf[0:HALO, :] = yp
    ybuf[HALO:HALO + tt, :] = ym
    ybuf[HALO + tt:, :] = yn
    return ym


def _store_pre(pre, res, c0):
    rows = res.shape[0]
    for s in range(res.shape[1] // LANES):
        pre[c0 // LANES + s, pl.ds(0, rows, stride=PRE_PITCH), :] = res[:, s * LANES:(s + 1) * LANES]


def _conv_tile(pre, cw_ref, r0, nrows, slab):
    acc = None
    for k in range(CONV_WIDTH):
        start = PRE_PITCH * (HALO - 1 + k + r0)
        tap = pre[slab, pl.ds(start, nrows, stride=PRE_PITCH), :]
        term = cw_ref[k:k + 1, slab * LANES:(slab + 1) * LANES] * tap
        acc = term if acc is None else acc + term
    return acc


def _rg_in_kernel(xp_ref, xm_ref, xn_ref, meta_ref, gain_ref, wx_ref, wz_ref, cw_ref, cb_ref,
                  xb_ref, zb_ref, ybuf, pre, *, tt, nt, npad):
    i = pl.program_id(1)
    hm = jnp.where(i == 0, meta_ref[...], xm_ref[0])
    hp = jnp.where(i == 1, meta_ref[tt - HALO:, :], xp_ref[0])
    ym = _fill_ybuf(hp, hm, xn_ref[0], gain_ref[...], ybuf, i, nt, tt)
    width = xb_ref.shape[-1]
    PROJ_BLOCK = RG_PROJ_BLOCK
    nblk = width // PROJ_BLOCK
    zblk = zb_ref.shape[-1] // nblk
    for j in range(nblk + 1):
        if j < nblk:
            c0 = j * PROJ_BLOCK
            _store_pre(pre, _nn(ybuf[...], wx_ref[:, c0:c0 + PROJ_BLOCK]), c0)
            zb_ref[0, :, j * zblk:(j + 1) * zblk] = _nn(ym, wz_ref[:, j * zblk:(j + 1) * zblk]).astype(ACT)
        if j > 0:
            for slab in range((j - 1) * PROJ_BLOCK // LANES, j * PROJ_BLOCK // LANES):
                lanes = slice(slab * LANES, (slab + 1) * LANES)
                for r0 in range(0, tt, CONV_ROWS):
                    xb_ref[0, r0:r0 + CONV_ROWS, lanes] = (
                        _conv_tile(pre, cw_ref, r0, CONV_ROWS, slab) + cb_ref[:, lanes]).astype(ACT)

    @pl.when(i == 0)
    def _():
        xb_ref[0, :npad, :] = jnp.zeros((npad, width), ACT)


def _rg_scan_kernel(xf_ref, xr_ref, gw_ref, gb_ref, lam_ref, hf_ref, hr_ref,
                    af_scr, bf_scr, ar_scr, br_scr, of_scr, or_scr, h_scr, *, tt, pitch):
    i = pl.program_id(1)

    @pl.when(i == 0)
    def _():
        h_scr[...] = jnp.zeros_like(h_scr)

    width = xf_ref.shape[-1]
    nblk = width // RG_BLOCK
    for d, (x_ref, a_scr, b_scr) in enumerate(((xf_ref, af_scr, bf_scr), (xr_ref, ar_scr, br_scr))):
        for n in range(nblk):
            c0 = n * RG_BLOCK
            xb16 = x_ref[0, :, c0:c0 + RG_BLOCK].astype(BF16)
            x = xb16.astype(F32)
            tr = jnp.tanh(_nn(xb16, gw_ref[d, 0, n]) + gb_ref[d, 0, :, c0:c0 + RG_BLOCK])
            ti = jnp.tanh(_nn(xb16, gw_ref[d, 1, n]) + gb_ref[d, 1, :, c0:c0 + RG_BLOCK])
            kk = (0.5 * RG_C) * _softplus(-lam_ref[d, :, c0:c0 + RG_BLOCK])
            neg_log_a = kk * tr + kk
            a = jnp.exp(-neg_log_a)
            y = jnp.tanh(neg_log_a) * (a * a + 1.0)
            mult = y * lax.rsqrt(jnp.maximum(y, F32_TINY))
            b = (mult * x) * (0.5 * ti + 0.5)
            for s in range(RG_BLOCK // LANES):
                q = n * (RG_BLOCK // LANES) + s
                a_scr[q * pitch:q * pitch + tt, :] = a[:, s * LANES:(s + 1) * LANES]
                b_scr[q * pitch:q * pitch + tt, :] = b[:, s * LANES:(s + 1) * LANES]

    nslab = width // LANES
    ngrp = nslab // 8

    def two_steps(a_scr, b_scr, o_scr, base, t0, t1, h):
        i0 = pl.ds(base + t0, 8, stride=pitch)
        i1 = pl.ds(base + t1, 8, stride=pitch)
        a0, b0, a1, b1 = a_scr[i0, :], b_scr[i0, :], a_scr[i1, :], b_scr[i1, :]
        o_scr[i0, :] = a0 * h + b0
        h1 = (a1 * a0) * h + (a1 * b0 + b1)
        o_scr[i1, :] = h1
        return h1

    def body(j, carry):
        t = 2 * j
        new = [two_steps(af_scr, bf_scr, of_scr, g * 8 * pitch, t, t + 1, carry[g]) for g in range(ngrp)]
        new += [two_steps(ar_scr, br_scr, or_scr, g * 8 * pitch, tt - 1 - t, tt - 2 - t, carry[ngrp + g])
                for g in range(ngrp)]
        return tuple(new)

    init = tuple(h_scr[j] for j in range(2 * ngrp))
    final = lax.fori_loop(0, tt // 2, body, init, unroll=SCAN_UNROLL // 2)
    for j in range(2 * ngrp):
        h_scr[j] = final[j]
    for q in range(nslab):
        hf_ref[0, :, q * LANES:(q + 1) * LANES] = of_scr[q * pitch:q * pitch + tt, :].astype(ACT)
        hr_ref[0, :, q * LANES:(q + 1) * LANES] = or_scr[q * pitch:q * pitch + tt, :].astype(ACT)


def _rg_out_kernel(hf_ref, hr_ref, z_ref, x_ref, meta_ref, wo_ref, gain_ref, out_ref):
    h0 = jnp.where(pl.program_id(1) == 0, meta_ref[...], x_ref[0])
    z = z_ref[0].astype(F32)
    o = (hf_ref[0].astype(F32) + hr_ref[0].astype(F32)) * _silu(z)
    y = _nn(o.astype(BF16), wo_ref[...])
    out_ref[0] = h0 + _rms(y, gain_ref[...])


def _dn_in_kernel(hp_ref, hm_ref, hn_ref, gain_ref, wqkv_ref, wz_ref, wab_ref, wabt_ref, cw_ref,
                  alog_row_ref, dtb_row_ref, alog_col_ref, dtb_col_ref, lmat_ref, umat_ref,
                  q_ref, k_ref, v_ref, z_ref, gb_ref, gct_ref, ybuf, pre, *, tt, nt, npad, qk_width):
    i = pl.program_id(1)
    ym = _fill_ybuf(hp_ref[0], hm_ref[0], hn_ref[0], gain_ref[...], ybuf, i, nt, tt)
    row = i * tt + lax.broadcasted_iota(jnp.int32, (tt, 1), 0)
    valid = row >= npad
    scale = DN_HEAD_DIM ** -0.5
    v_width = v_ref.shape[-1]
    PROJ_BLOCK = DN_PROJ_BLOCK
    nblk = (2 * qk_width + v_width) // PROJ_BLOCK
    zblk = z_ref.shape[-1] // nblk

    lane = lax.broadcasted_iota(jnp.int32, (1, LANES), 1)
    nheads = gct_ref.shape[1] // 2
    for d in range(2):
        ab = _nn(ym, wab_ref[d])
        g = -jnp.exp(alog_row_ref[d]) * _softplus(ab + dtb_row_ref[d])
        g = jnp.where(valid, g, 0.0)
        g_hi, g_lo = _split_hi_lo(g)
        cmat = lmat_ref[...] if d == 0 else umat_ref[...]
        gc = _nn(cmat, g_hi) + _nn(cmat, g_lo)
        beta = jnp.where(valid, _sigmoid(ab), 0.0)
        gb_ref[d, 0] = jnp.where(lane < nheads, gc, beta)

    col = i * tt + lax.broadcasted_iota(jnp.int32, (1, tt), 1)
    abt = _nt(wabt_ref[...], ym)
    gt = -jnp.exp(alog_col_ref[...]) * _softplus(abt + dtb_col_ref[...])
    gt = jnp.where(col >= npad, gt, 0.0)
    gt_hi, gt_lo = _split_hi_lo(gt)
    gct_f = _nn(gt_hi[:nheads], umat_ref[...]) + _nn(gt_lo[:nheads], umat_ref[...])
    gct_r = _nn(gt_hi[nheads:], lmat_ref[...]) + _nn(gt_lo[nheads:], lmat_ref[...])
    gct_ref[0, :nheads, :] = gct_f
    gct_ref[0, nheads:, :] = gct_r

    def consume(c0):
        if c0 < 2 * qk_width:
            for h0 in range(c0, c0 + PROJ_BLOCK, DN_HEAD_DIM):
                for r0 in range(0, tt, CONV_ROWS):
                    acc = _silu(_conv_tile(pre, cw_ref, r0, CONV_ROWS, h0 // LANES))
                    inv = lax.rsqrt(jnp.sum(acc * acc, axis=-1, keepdims=True) + NORM_EPS)
                    if h0 < qk_width:
                        q_ref[0, r0:r0 + CONV_ROWS, h0:h0 + DN_HEAD_DIM] = (acc * (inv * scale)).astype(ACT)
                    else:
                        k0 = h0 - qk_width
                        k_ref[0, r0:r0 + CONV_ROWS, k0:k0 + DN_HEAD_DIM] = (acc * inv).astype(ACT)
        else:
            for slab in range(c0 // LANES, (c0 + PROJ_BLOCK) // LANES):
                v0 = slab * LANES - 2 * qk_width
                for r0 in range(0, tt, CONV_ROWS):
                    v_ref[0, r0:r0 + CONV_ROWS, v0:v0 + LANES] = _silu(
                        _conv_tile(pre, cw_ref, r0, CONV_ROWS, slab)).astype(ACT)

    for j in range(nblk + 1):
        if j < nblk:
            c0 = j * PROJ_BLOCK
            _store_pre(pre, _nn(ybuf[...], wqkv_ref[:, c0:c0 + PROJ_BLOCK]), c0)
            z_ref[0, :, j * zblk:(j + 1) * zblk] = _nn(ym, wz_ref[:, j * zblk:(j + 1) * zblk]).astype(ACT)
        if j > 0:
            consume((j - 1) * PROJ_BLOCK)

    @pl.when(i == 0)
    def _():
        q_ref[0, :npad, :] = jnp.zeros((npad, qk_width), ACT)
        k_ref[0, :npad, :] = jnp.zeros((npad, qk_width), ACT)
        v_ref[0, :npad, :] = jnp.zeros((npad, v_width), ACT)


def _delta_kernel(q_ref, k_ref, v_ref, gb_ref, gct_ref, o_ref, s_scr, *, nheads, rep):
    d = pl.program_id(1)
    c = pl.program_id(2)

    @pl.when(c == 0)
    def _():
        s_scr[...] = jnp.zeros_like(s_scr)

    cs = q_ref.shape[1]
    rows = lax.broadcasted_iota(jnp.int32, (cs, cs), 0)
    cols = lax.broadcasted_iota(jnp.int32, (cs, cs), 1)
    fwd = d == 0
    later = jnp.where(fwd, rows, cols)
    earlier = jnp.where(fwd, cols, rows)
    incl = later >= earlier
    strict = later > earlier
    block_xor = rows ^ cols
    base_mask = block_xor < INV_BASE
    band_masks = []
    size = INV_BASE
    while size < cs:
        band_masks.append((block_xor >> (size.bit_length() - 1)) == 1)
        size *= 2

    gb = gb_ref[0, 0]
    gct = gct_ref[0]
    gtot_row = jnp.where(fwd, gb[cs - 1:cs, :], gb[0:1, :])
    gdiff = gtot_row - gb

    hd = DN_HEAD_DIM
    for g0 in range(0, nheads, DELTA_HEAD_GROUP):
        heads = list(range(g0, g0 + DELTA_HEAD_GROUP))
        pairs = list(range(g0 // 2, (g0 + DELTA_HEAD_GROUP) // 2))
        qp = {p: q_ref[0, :, p * hd:(p + 1) * hd] for p in pairs}
        kp = {p: k_ref[0, :, p * hd:(p + 1) * hd] for p in pairs}
        kp16 = {p: kp[p].astype(BF16) for p in pairs}
        kk = {p: _nt(kp16[p], kp16[p]) for p in pairs}
        qk0 = {p: _nt(qp[p].astype(BF16), kp16[p]) for p in pairs}
        gcol = {h: gb[:, h:h + 1] for h in heads}
        bcol = {h: gb[:, nheads + h:nheads + h + 1] for h in heads}
        decay = {h: jnp.exp(jnp.where(incl, gcol[h] - gct[h:h + 1, :], -1e30)) for h in heads}
        a_mat = {h: jnp.where(strict, kk[h // 2] * decay[h], 0.0) * bcol[h] for h in heads}
        qkd16 = {p: _pair(*[(qk0[p] * decay[h]).astype(BF16) for h in (2 * p, 2 * p + 1)]) for p in pairs}
        dm = {h: -jnp.where(base_mask, a_mat[h], 0.0) for h in heads}
        qk = dm
        span = 1
        while span < INV_BASE:
            qk16 = {h: qk[h].astype(BF16) for h in heads}
            span *= 2
            if span == 2:
                qk = {h: _nn(qk16[h], qk16[h]) for h in heads}
                continue
            if span < INV_BASE:
                both = {h: _nn(qk16[h], _pair(qk16[h], dm[h].astype(BF16))) for h in heads}
                dm = {h: dm[h] + qk[h] + both[h][:, cs:] for h in heads}
                qk = {h: both[h][:, :cs] for h in heads}
            else:
                dm = {h: dm[h] + qk[h] + _nn(qk16[h], dm[h].astype(BF16)) for h in heads}
        for band in band_masks:
            ao = {h: jnp.where(band, a_mat[h], 0.0) for h in heads}
            ya = {h: ao[h] + _nn(dm[h].astype(BF16), ao[h].astype(BF16)) for h in heads}
            dm = {h: dm[h] - ya[h] - _nn(ya[h].astype(BF16), dm[h].astype(BF16)) for h in heads}
        egc = {h: jnp.exp(gcol[h]) for h in heads}
        uw = {}
        for h in heads:
            rhs = _pair(v_ref[0, :, h * hd:(h + 1) * hd] * bcol[h], kp[h // 2] * (bcol[h] * egc[h]))
            uw[h] = rhs + _nn(dm[h].astype(BF16), rhs.astype(BF16))
        s = {h: s_scr[h] for h in heads}
        s_bd = {p: _block_diag(s[2 * p].astype(BF16), s[2 * p + 1].astype(BF16)) for p in pairs}
        wq = {}
        for p in pairs:
            w_cat = _pair(uw[2 * p][:, hd:], uw[2 * p + 1][:, hd:]).astype(BF16)
            qd_cat = _pair(qp[p] * egc[2 * p], qp[p] * egc[2 * p + 1]).astype(BF16)
            wq[p] = _nn(jnp.concatenate([w_cat, qd_cat], axis=0), s_bd[p])
        vn16 = {p: (_pair(uw[2 * p][:, :hd], uw[2 * p + 1][:, :hd]) - wq[p][:cs]).astype(BF16) for p in pairs}
        for p in pairs:
            vn_bd = _block_diag(vn16[p][:, :hd], vn16[p][:, hd:])
            o_ref[0, 0, :, 2 * p * hd:(2 * p + 2) * hd] = (wq[p][cs:] + _nn(qkd16[p], vn_bd)).astype(ACT)
        for h in heads:
            kt = kp[h // 2] * jnp.exp(gdiff[:, h:h + 1])
            gtot = jnp.exp(jnp.where(fwd, gcol[h][cs - 1:cs, :], gcol[h][0:1, :]))
            vn_h = vn16[h // 2][:, (h % 2) * hd:(h % 2 + 1) * hd]
            s_scr[h] = s[h] * gtot + _tn(kt.astype(BF16), vn_h)


def _dn_out_kernel(of_ref, or_ref, z_ref, h1_ref, nw_ref, wo_ref, gain_ref, out_ref, ybuf):
    width = z_ref.shape[-1]
    for c0 in range(0, width, DN_HEAD_DIM):
        c1 = c0 + DN_HEAD_DIM
        o = of_ref[0, 0, :, c0:c1].astype(F32) + or_ref[0, 0, :, c0:c1].astype(F32)
        z = z_ref[0, :, c0:c1].astype(F32)
        y = o * lax.rsqrt(jnp.mean(o * o, axis=-1, keepdims=True) + NORM_EPS) * nw_ref[...]
        ybuf[:, c0:c1] = (y * _silu(z)).astype(BF16)
    y = _nn(ybuf[...], wo_ref[...])
    out_ref[0] = h1_ref[0] + _rms(y, gain_ref[...])


def _params(sem):
    return pltpu.CompilerParams(dimension_semantics=sem, vmem_limit_bytes=VMEM_LIMIT)


def _const_spec(shape):
    nd = len(shape)
    return pl.BlockSpec(shape, lambda *_: (0,) * nd)


def _halo_specs(tt, nt, d_model):
    per = tt // HALO
    last = nt * per - 1
    prev = pl.BlockSpec((1, HALO, d_model), lambda b, i: (b, jnp.maximum(i * per - 1, 0), 0))
    main = pl.BlockSpec((1, tt, d_model), lambda b, i: (b, i, 0))
    nxt = pl.BlockSpec((1, HALO, d_model), lambda b, i: (b, jnp.minimum((i + 1) * per, last), 0))
    return prev, main, nxt


def kernel(x, meta_tokens, norm_w, rg_w_in, rg_conv_w, rg_conv_b, rg_gate_w, rg_gate_b, rg_lambda,
           rg_w_out, dn_w_in, dn_conv_w, dn_a_log, dn_dt_bias, dn_norm_w, dn_w_out):
    bsz, seq, d_model = x.shape
    n_meta = meta_tokens.shape[0]
    tt = TIME_TILE
    assert seq % tt == 0 and n_meta <= tt and tt % CHUNK == 0
    npad = tt - n_meta
    tf = tt + seq
    nt = tf // tt
    rg_width = rg_w_out.shape[1]
    v_width = dn_w_out.shape[1]
    nheads = dn_a_log.shape[-1]
    qk_width = (dn_conv_w.shape[-1] - v_width) // 2
    rep = v_width // qk_width
    conv_ch = 2 * qk_width + v_width
    assert rg_width % (8 * LANES) == 0 and nheads * DN_HEAD_DIM == v_width and 2 * nheads <= LANES

    meta_tile = jnp.concatenate([jnp.zeros((npad, d_model), F32), meta_tokens.astype(F32)], axis=0)

    prev_spec, main_spec, next_spec = _halo_specs(tt, nt, d_model)
    per = tt // HALO
    x_prev = pl.BlockSpec((1, HALO, d_model), lambda b, i: (b, jnp.maximum((i - 1) * per - 1, 0), 0))
    x_main = pl.BlockSpec((1, tt, d_model), lambda b, i: (b, jnp.maximum(i - 1, 0), 0))
    x_next = pl.BlockSpec((1, HALO, d_model), lambda b, i: (b, jnp.minimum(i * per, (nt - 1) * per - 1), 0))
    tile = lambda w: pl.BlockSpec((1, tt, w), lambda b, i: (b, i, 0))
    rtile = lambda w: pl.BlockSpec((1, tt, w), lambda b, i: (b, nt - 1 - i, 0))

    w_in = rg_w_in[0].astype(BF16)
    xb, zb = pl.pallas_call(
        functools.partial(_rg_in_kernel, tt=tt, nt=nt, npad=npad),
        grid=(bsz, nt),
        in_specs=[x_prev, x_main, x_next, _const_spec((tt, d_model)), _const_spec((1, d_model)),
                  _const_spec((d_model, rg_width)), _const_spec((d_model, rg_width)),
                  _const_spec((CONV_WIDTH, rg_width)), _const_spec((1, rg_width))],
        out_specs=[tile(rg_width), tile(rg_width)],
        out_shape=[jax.ShapeDtypeStruct((bsz, tf, rg_width), ACT)] * 2,
        scratch_shapes=[pltpu.VMEM((tt + 2 * HALO, d_model), BF16),
                        pltpu.VMEM((rg_width // LANES, PRE_PITCH * (tt + 2 * HALO), LANES), F32)],
        compiler_params=_params(("parallel", "arbitrary")),
        name="rg_in",
    )(x, x, x, meta_tile, norm_w[0, 0][None], w_in[:, :rg_width], w_in[:, rg_width:],
      rg_conv_w[0], rg_conv_b[0][None])

    pitch = tt + SCAN_SLAB_PAD
    nslab = rg_width // LANES
    gate_w = (0.5 * rg_gate_w[0]).astype(BF16)
    gate_b = 0.5 * rg_gate_b[0].reshape(2, 2, 1, rg_width)
    slab = pltpu.VMEM((nslab * pitch, LANES), F32)
    hf, hr = pl.pallas_call(
        functools.partial(_rg_scan_kernel, tt=tt, pitch=pitch),
        grid=(bsz, nt),
        in_specs=[tile(rg_width), rtile(rg_width), _const_spec(gate_w.shape),
                  _const_spec(gate_b.shape), _const_spec((2, 1, rg_width))],
        out_specs=[tile(rg_width), rtile(rg_width)],
        out_shape=[jax.ShapeDtypeStruct((bsz, tf, rg_width), ACT)] * 2,
        scratch_shapes=[slab] * 6 + [pltpu.VMEM((2 * nslab // 8, 8, LANES), F32)],
        compiler_params=_params(("parallel", "arbitrary")),
        name="rg_scan",
    )(xb, xb, gate_w, gate_b, rg_lambda[0][:, None, :])

    h1 = pl.pallas_call(
        _rg_out_kernel,
        grid=(bsz, nt),
        in_specs=[tile(rg_width), tile(rg_width), tile(rg_width), x_main, _const_spec((tt, d_model)),
                  _const_spec((rg_width, d_model)), _const_spec((1, d_model))],
        out_specs=tile(d_model),
        out_shape=jax.ShapeDtypeStruct((bsz, tf, d_model), F32),
        compiler_params=_params(("parallel", "parallel")),
        name="rg_out",
    )(hf, hr, zb, x, meta_tile, rg_w_out[0].astype(BF16), norm_w[0, 1][None])

    w_in = dn_w_in[0]
    w_ab = w_in[:, conv_ch + v_width:]
    w_a = w_ab[:, :2 * nheads].reshape(d_model, 2, nheads)
    w_b = w_ab[:, 2 * nheads:].reshape(d_model, 2, nheads)
    w_ab_nat = jnp.concatenate(
        [w_a, w_b, jnp.zeros((d_model, 2, LANES - 2 * nheads), F32)], axis=-1)
    w_ab_nat = jnp.moveaxis(w_ab_nat, 1, 0).astype(BF16)
    w_ab_t = w_ab[:, :2 * nheads].T.astype(BF16)
    lane_pad = lambda p: jnp.pad(p, ((0, 0), (0, LANES - nheads)))[:, None, :]
    ti = jnp.arange(tt)
    lmat = ((ti[:, None] >= ti[None, :]) & (ti[:, None] // CHUNK == ti[None, :] // CHUNK)).astype(BF16)
    umat = lmat.T

    q, k, v, z, gb, gct = pl.pallas_call(
        functools.partial(_dn_in_kernel, tt=tt, nt=nt, npad=npad, qk_width=qk_width),
        grid=(bsz, nt),
        in_specs=[prev_spec, main_spec, next_spec, _const_spec((1, d_model)),
                  _const_spec((d_model, conv_ch)), _const_spec((d_model, v_width)),
                  _const_spec((2, d_model, LANES)), _const_spec((2 * nheads, d_model)),
                  _const_spec((CONV_WIDTH, conv_ch)),
                  _const_spec((2, 1, LANES)), _const_spec((2, 1, LANES)),
                  _const_spec((2 * nheads, 1)), _const_spec((2 * nheads, 1)),
                  _const_spec((tt, tt)), _const_spec((tt, tt))],
        out_specs=[tile(qk_width), tile(qk_width), tile(v_width), tile(v_width),
                   pl.BlockSpec((2, 1, tt, LANES), lambda b, i: (0, b, i, 0)),
                   pl.BlockSpec((1, 2 * nheads, tt), lambda b, i: (b, 0, i))],
        out_shape=[jax.ShapeDtypeStruct((bsz, tf, qk_width), ACT),
                   jax.ShapeDtypeStruct((bsz, tf, qk_width), ACT),
                   jax.ShapeDtypeStruct((bsz, tf, v_width), ACT),
                   jax.ShapeDtypeStruct((bsz, tf, v_width), ACT),
                   jax.ShapeDtypeStruct((2, bsz, tf, LANES), F32),
                   jax.ShapeDtypeStruct((bsz, 2 * nheads, tf), F32)],
        scratch_shapes=[pltpu.VMEM((tt + 2 * HALO, d_model), BF16),
                        pltpu.VMEM((conv_ch // LANES, PRE_PITCH * (tt + 2 * HALO), LANES), F32)],
        compiler_params=_params(("parallel", "arbitrary")),
        name="dn_in",
    )(h1, h1, h1, norm_w[1, 0][None], w_in[:, :conv_ch].astype(BF16),
      w_in[:, conv_ch:conv_ch + v_width].astype(BF16), w_ab_nat, w_ab_t, dn_conv_w[0],
      lane_pad(dn_a_log[0]), lane_pad(dn_dt_bias[0]),
      dn_a_log[0].reshape(2 * nheads, 1), dn_dt_bias[0].reshape(2 * nheads, 1), lmat, umat)

    nch = tf // CHUNK
    cidx = lambda d, c: c + d * (nch - 1 - 2 * c)
    ctile = lambda w: pl.BlockSpec((1, CHUNK, w), lambda b, d, c: (b, cidx(d, c), 0))
    o2 = pl.pallas_call(
        functools.partial(_delta_kernel, nheads=nheads, rep=rep),
        grid=(bsz, 2, nch),
        in_specs=[ctile(qk_width), ctile(qk_width), ctile(v_width),
                  pl.BlockSpec((1, 1, CHUNK, LANES), lambda b, d, c: (d, b, cidx(d, c), 0)),
                  pl.BlockSpec((1, nheads, CHUNK), lambda b, d, c: (b, d, cidx(d, c)))],
        out_specs=pl.BlockSpec((1, 1, CHUNK, v_width), lambda b, d, c: (d, b, cidx(d, c), 0)),
        out_shape=jax.ShapeDtypeStruct((2, bsz, tf, v_width), ACT),
        scratch_shapes=[pltpu.VMEM((nheads, DN_HEAD_DIM, DN_HEAD_DIM), F32)],
        compiler_params=_params(("parallel", "parallel", "arbitrary")),
        name="delta",
    )(q, k, v, gb, gct)

    xtile = lambda w: pl.BlockSpec((1, tt, w), lambda b, i: (b, i + 1, 0))
    out = pl.pallas_call(
        _dn_out_kernel,
        grid=(bsz, nt - 1),
        in_specs=[pl.BlockSpec((1, 1, tt, v_width), lambda b, i: (0, b, i + 1, 0)),
                  pl.BlockSpec((1, 1, tt, v_width), lambda b, i: (1, b, i + 1, 0)),
                  xtile(v_width), xtile(d_model), _const_spec((1, DN_HEAD_DIM)),
                  _const_spec((v_width, d_model)), _const_spec((1, d_model))],
        out_specs=pl.BlockSpec((1, tt, d_model), lambda b, i: (b, i, 0)),
        out_shape=jax.ShapeDtypeStruct((bsz, seq, d_model), F32),
        scratch_shapes=[pltpu.VMEM((tt, v_width), BF16)],
        compiler_params=_params(("parallel", "parallel")),
        name="dn_out",
    )(o2, o2, z, h1, dn_norm_w[0][None], dn_w_out[0].astype(BF16), norm_w[1, 1][None])
    return out
```

```python
import functools

import jax
import jax.numpy as jnp
from jax import lax
from jax.experimental import pallas as pl
from jax.experimental.pallas import tpu as pltpu

F32 = jnp.float32
BF16 = jnp.bfloat16
ACT = jnp.bfloat16

NORM_EPS = 1e-6
RG_C = 8.0
RG_BLOCK = 256
DN_HEAD_DIM = 128
CONV_WIDTH = 4

TIME_TILE = 256
HALO = 16
CHUNK = 128
LANES = 128
SCAN_SLAB_PAD = 4
RG_PROJ_BLOCK = 256
DN_PROJ_BLOCK = 1024
CONV_ROWS = 64
PRE_PITCH = 2
SCAN_UNROLL = 16
F32_TINY = 1.1754944e-38
DELTA_HEAD_GROUP = 16
DELTA_BLOCK = 256
DELTA_CHUNK_SKEW = 5
INV_BASE = 8
VMEM_LIMIT = 56 * 1024 * 1024


def _nn(a, b):
    return jnp.dot(a, b, preferred_element_type=F32)


def _nt(a, b):
    return lax.dot_general(a, b, (((1,), (1,)), ((), ())), preferred_element_type=F32)


def _tn(a, b):
    return lax.dot_general(a, b, (((0,), (0,)), ((), ())), preferred_element_type=F32)


def _pair(a, b):
    return jnp.concatenate([a, b], axis=1)


def _block_diag(a, b):
    za = jnp.zeros((a.shape[0], b.shape[1]), a.dtype)
    zb = jnp.zeros((b.shape[0], a.shape[1]), a.dtype)
    return jnp.concatenate([_pair(a, za), _pair(zb, b)], axis=0)


def _sigmoid(x):
    return 0.5 * jnp.tanh(0.5 * x) + 0.5


def _silu(x):
    h = 0.5 * x
    return h * jnp.tanh(h) + h


def _softplus(x):
    return jnp.maximum(x, 0.0) + jnp.log1p(jnp.exp(-jnp.abs(x)))


def _rms(h, gain):
    return h * lax.rsqrt(jnp.mean(h * h, axis=-1, keepdims=True) + NORM_EPS) * gain


def _split_hi_lo(x):
    hi = x.astype(BF16)
    lo = (x - hi.astype(F32)).astype(BF16)
    return hi, lo


def _fill_ybuf(hp, hm, hn, gain, ybuf, i, nt, tt):
    ym = _rms(hm, gain).astype(BF16)
    yp = jnp.where(i > 0, _rms(hp, gain), 0.0).astype(BF16)
    yn = jnp.where(i < nt - 1, _rms(hn, gain), 0.0).astype(BF16)
    ybuf[0:HALO, :] = yp
    ybuf[HALO:HALO + tt, :] = ym
    ybuf[HALO + tt:, :] = yn
    return ym


def _store_pre(pre, res, c0):
    rows = res.shape[0]
    for s in range(res.shape[1] // LANES):
        pre[c0 // LANES + s, pl.ds(0, rows, stride=PRE_PITCH), :] = res[:, s * LANES:(s + 1) * LANES]


def _conv_tile(pre, cw_ref, r0, nrows, slab):
    acc = None
    for k in range(CONV_WIDTH):
        start = PRE_PITCH * (HALO - 1 + k + r0)
        tap = pre[slab, pl.ds(start, nrows, stride=PRE_PITCH), :]
        term = cw_ref[k:k + 1, slab * LANES:(slab + 1) * LANES] * tap
        acc = term if acc is None else acc + term
    return acc


def _rg_in_kernel(xp_ref, xm_ref, xn_ref, meta_ref, gain_ref, wx_ref, wz_ref, cw_ref, cb_ref,
                  xb_ref, zb_ref, ybuf, pre, *, tt, nt, npad):
    i = pl.program_id(1)
    hm = jnp.where(i == 0, meta_ref[...], xm_ref[0])
    hp = jnp.where(i == 1, meta_ref[tt - HALO:, :], xp_ref[0])
    ym = _fill_ybuf(hp, hm, xn_ref[0], gain_ref[...], ybuf, i, nt, tt)
    width = xb_ref.shape[-1]
    PROJ_BLOCK = RG_PROJ_BLOCK
    nblk = width // PROJ_BLOCK
    zblk = zb_ref.shape[-1] // nblk
    for j in range(nblk + 1):
        if j < nblk:
            c0 = j * PROJ_BLOCK
            _store_pre(pre, _nn(ybuf[...], wx_ref[:, c0:c0 + PROJ_BLOCK]), c0)
            zb_ref[0, :, j * zblk:(j + 1) * zblk] = _nn(ym, wz_ref[:, j * zblk:(j + 1) * zblk]).astype(ACT)
        if j > 0:
            for slab in range((j - 1) * PROJ_BLOCK // LANES, j * PROJ_BLOCK // LANES):
                lanes = slice(slab * LANES, (slab + 1) * LANES)
                for r0 in range(0, tt, CONV_ROWS):
                    xb_ref[0, r0:r0 + CONV_ROWS, lanes] = (
                        _conv_tile(pre, cw_ref, r0, CONV_ROWS, slab) + cb_ref[:, lanes]).astype(ACT)

    @pl.when(i == 0)
    def _():
        xb_ref[0, :npad, :] = jnp.zeros((npad, width), ACT)


def _rg_scan_kernel(xf_ref, xr_ref, gw_ref, gb_ref, lam_ref, hf_ref, hr_ref,
                    af_scr, bf_scr, ar_scr, br_scr, of_scr, or_scr, h_scr, *, tt, pitch):
    i = pl.program_id(1)

    @pl.when(i == 0)
    def _():
        h_scr[...] = jnp.zeros_like(h_scr)

    width = xf_ref.shape[-1]
    nblk = width // RG_BLOCK
    for d, (x_ref, a_scr, b_scr) in enumerate(((xf_ref, af_scr, bf_scr), (xr_ref, ar_scr, br_scr))):
        for n in range(nblk):
            c0 = n * RG_BLOCK
            xb16 = x_ref[0, :, c0:c0 + RG_BLOCK].astype(BF16)
            x = xb16.astype(F32)
            tr = jnp.tanh(_nn(xb16, gw_ref[d, 0, n]) + gb_ref[d, 0, :, c0:c0 + RG_BLOCK])
            ti = jnp.tanh(_nn(xb16, gw_ref[d, 1, n]) + gb_ref[d, 1, :, c0:c0 + RG_BLOCK])
            kk = (0.5 * RG_C) * _softplus(-lam_ref[d, :, c0:c0 + RG_BLOCK])
            neg_log_a = kk * tr + kk
            a = jnp.exp(-neg_log_a)
            y = jnp.tanh(neg_log_a) * (a * a + 1.0)
            mult = y * lax.rsqrt(jnp.maximum(y, F32_TINY))
            b = (mult * x) * (0.5 * ti + 0.5)
            for s in range(RG_BLOCK // LANES):
                q = n * (RG_BLOCK // LANES) + s
                a_scr[q * pitch:q * pitch + tt, :] = a[:, s * LANES:(s + 1) * LANES]
                b_scr[q * pitch:q * pitch + tt, :] = b[:, s * LANES:(s + 1) * LANES]

    nslab = width // LANES
    ngrp = nslab // 8

    def two_steps(a_scr, b_scr, o_scr, base, t0, t1, h):
        i0 = pl.ds(base + t0, 8, stride=pitch)
        i1 = pl.ds(base + t1, 8, stride=pitch)
        a0, b0, a1, b1 = a_scr[i0, :], b_scr[i0, :], a_scr[i1, :], b_scr[i1, :]
        o_scr[i0, :] = a0 * h + b0
        h1 = (a1 * a0) * h + (a1 * b0 + b1)
        o_scr[i1, :] = h1
        return h1

    def body(j, carry):
        t = 2 * j
        new = [two_steps(af_scr, bf_scr, of_scr, g * 8 * pitch, t, t + 1, carry[g]) for g in range(ngrp)]
        new += [two_steps(ar_scr, br_scr, or_scr, g * 8 * pitch, tt - 1 - t, tt - 2 - t, carry[ngrp + g])
                for g in range(ngrp)]
        return tuple(new)

    init = tuple(h_scr[j] for j in range(2 * ngrp))
    final = lax.fori_loop(0, tt // 2, body, init, unroll=SCAN_UNROLL // 2)
    for j in range(2 * ngrp):
        h_scr[j] = final[j]
    for q in range(nslab):
        hf_ref[0, :, q * LANES:(q + 1) * LANES] = of_scr[q * pitch:q * pitch + tt, :].astype(ACT)
        hr_ref[0, :, q * LANES:(q + 1) * LANES] = or_scr[q * pitch:q * pitch + tt, :].astype(ACT)


def _rg_out_kernel(hf_ref, hr_ref, z_ref, x_ref, meta_ref, wo_ref, gain_ref, out_ref):
    h0 = jnp.where(pl.program_id(1) == 0, meta_ref[...], x_ref[0])
    z = z_ref[0].astype(F32)
    o = (hf_ref[0].astype(F32) + hr_ref[0].astype(F32)) * _silu(z)
    y = _nn(o.astype(BF16), wo_ref[...])
    out_ref[0] = h0 + _rms(y, gain_ref[...])


def _dn_in_kernel(hp_ref, hm_ref, hn_ref, gain_ref, wqkv_ref, wz_ref, wab_ref, wabt_ref, cw_ref,
                  alog_row_ref, dtb_row_ref, alog_col_ref, dtb_col_ref, lmat_ref, umat_ref,
                  q_ref, k_ref, v_ref, z_ref, gb_ref, gct_ref, ybuf, pre, *, tt, nt, npad, qk_width):
    i = pl.program_id(1)
    ym = _fill_ybuf(hp_ref[0], hm_ref[0], hn_ref[0], gain_ref[...], ybuf, i, nt, tt)
    row = i * tt + lax.broadcasted_iota(jnp.int32, (tt, 1), 0)
    valid = row >= npad
    scale = DN_HEAD_DIM ** -0.5
    v_width = v_ref.shape[-1]
    PROJ_BLOCK = DN_PROJ_BLOCK
    nblk = (2 * qk_width + v_width) // PROJ_BLOCK
    zblk = z_ref.shape[-1] // nblk

    lane = lax.broadcasted_iota(jnp.int32, (1, LANES), 1)
    nheads = gct_ref.shape[1] // 2
    for d in range(2):
        ab = _nn(ym, wab_ref[d])
        g = -jnp.exp(alog_row_ref[d]) * _softplus(ab + dtb_row_ref[d])
        g = jnp.where(valid, g, 0.0)
        g_hi, g_lo = _split_hi_lo(g)
        cmat = lmat_ref[...] if d == 0 else umat_ref[...]
        gc = _nn(cmat, g_hi) + _nn(cmat, g_lo)
        beta = jnp.where(valid, _sigmoid(ab), 0.0)
        gb_ref[d, 0] = jnp.where(lane < nheads, gc, beta)

    col = i * tt + lax.broadcasted_iota(jnp.int32, (1, tt), 1)
    abt = _nt(wabt_ref[...], ym)
    gt = -jnp.exp(alog_col_ref[...]) * _softplus(abt + dtb_col_ref[...])
    gt = jnp.where(col >= npad, gt, 0.0)
    gt_hi, gt_lo = _split_hi_lo(gt)
    gct_f = _nn(gt_hi[:nheads], umat_ref[...]) + _nn(gt_lo[:nheads], umat_ref[...])
    gct_r = _nn(gt_hi[nheads:], lmat_ref[...]) + _nn(gt_lo[nheads:], lmat_ref[...])
    gct_ref[0, :nheads, :] = gct_f
    gct_ref[0, nheads:, :] = gct_r

    def consume(c0):
        if c0 < 2 * qk_width:
            for h0 in range(c0, c0 + PROJ_BLOCK, DN_HEAD_DIM):
                for r0 in range(0, tt, CONV_ROWS):
                    acc = _silu(_conv_tile(pre, cw_ref, r0, CONV_ROWS, h0 // LANES))
                    inv = lax.rsqrt(jnp.sum(acc * acc, axis=-1, keepdims=True) + NORM_EPS)
                    if h0 < qk_width:
                        q_ref[0, r0:r0 + CONV_ROWS, h0:h0 + DN_HEAD_DIM] = (acc * (inv * scale)).astype(ACT)
                    else:
                        k0 = h0 - qk_width
                        k_ref[0, r0:r0 + CONV_ROWS, k0:k0 + DN_HEAD_DIM] = (acc * inv).astype(ACT)
        else:
            for slab in range(c0 // LANES, (c0 + PROJ_BLOCK) // LANES):
                v0 = slab * LANES - 2 * qk_width
                for r0 in range(0, tt, CONV_ROWS):
                    v_ref[0, r0:r0 + CONV_ROWS, v0:v0 + LANES] = _silu(
                        _conv_tile(pre, cw_ref, r0, CONV_ROWS, slab)).astype(ACT)

    for j in range(nblk + 1):
        if j < nblk:
            c0 = j * PROJ_BLOCK
            _store_pre(pre, _nn(ybuf[...], wqkv_ref[:, c0:c0 + PROJ_BLOCK]), c0)
            z_ref[0, :, j * zblk:(j + 1) * zblk] = _nn(ym, wz_ref[:, j * zblk:(j + 1) * zblk]).astype(ACT)
        if j > 0:
            consume((j - 1) * PROJ_BLOCK)

    @pl.when(i == 0)
    def _():
        q_ref[0, :npad, :] = jnp.zeros((npad, qk_width), ACT)
        k_ref[0, :npad, :] = jnp.zeros((npad, qk_width), ACT)
        v_ref[0, :npad, :] = jnp.zeros((npad, v_width), ACT)


def _delta_kernel(q_ref, k_ref, v_ref, gb_ref, gct_ref, o_ref, s_scr, *, nheads, reverse):
    c = pl.program_id(1)

    @pl.when(c == 0)
    def _():
        s_scr[...] = jnp.zeros_like(s_scr)

    cs = CHUNK
    rows = lax.broadcasted_iota(jnp.int32, (cs, cs), 0)
    cols = lax.broadcasted_iota(jnp.int32, (cs, cs), 1)
    incl = rows <= cols if reverse else rows >= cols
    strict = rows < cols if reverse else rows > cols
    base_mask = (rows ^ cols) < INV_BASE
    active = 0 if reverse else 1
    half_rows = lax.broadcasted_iota(jnp.int32, (cs // 2, cs), 0)
    half_cols = lax.broadcasted_iota(jnp.int32, (cs // 2, cs), 1)
    levels = []
    size = INV_BASE
    while size < cs:
        shift = size.bit_length() - 1
        levels.append((size, (half_cols >> shift) == 2 * (half_rows >> shift) + (1 - active)))
        size *= 2

    def take(x, half, size):
        return jnp.concatenate([x[b0 + half * size:b0 + (half + 1) * size] for b0 in range(0, cs, 2 * size)], axis=0)

    def merge(idle, act, size):
        pieces = []
        for n in range(cs // (2 * size)):
            pair = (idle[n * size:(n + 1) * size], act[n * size:(n + 1) * size])
            pieces += pair if active == 1 else pair[::-1]
        return jnp.concatenate(pieces, axis=0)

    edge = 0 if reverse else cs - 1

    hd = DN_HEAD_DIM
    updated = []

    def group_stages(g0, r0, position):
        tsl = slice(r0, r0 + cs)
        gb = gb_ref[0, 0, tsl, :]
        gct = gct_ref[0, :, tsl]
        gdiff = gb[edge:edge + 1, :] - gb
        heads = list(range(g0, g0 + DELTA_HEAD_GROUP))
        pairs = list(range(g0 // 2, (g0 + DELTA_HEAD_GROUP) // 2))
        qp = {p: q_ref[0, tsl, p * hd:(p + 1) * hd] for p in pairs}
        kp = {p: k_ref[0, tsl, p * hd:(p + 1) * hd] for p in pairs}
        kp16 = {p: kp[p].astype(BF16) for p in pairs}
        kk = {p: _nt(kp16[p], kp16[p]) for p in pairs}
        qk0 = {p: _nt(qp[p].astype(BF16), kp16[p]) for p in pairs}
        gcol = {h: gb[:, h:h + 1] for h in heads}
        bcol = {h: gb[:, nheads + h:nheads + h + 1] for h in heads}
        decay = {h: jnp.exp(jnp.where(incl, gcol[h] - gct[h:h + 1, :], -1e30)) for h in heads}
        a_mat = {h: jnp.where(strict, kk[h // 2] * decay[h], 0.0) * bcol[h] for h in heads}
        qkd16 = {p: _pair(*[(qk0[p] * decay[h]).astype(BF16) for h in (2 * p, 2 * p + 1)]) for p in pairs}
        yield
        dm = {h: -jnp.where(base_mask, a_mat[h], 0.0) for h in heads}
        qk = dm
        span = 1
        while span < INV_BASE:
            qk16 = {h: qk[h].astype(BF16) for h in heads}
            span *= 2
            if span == 2:
                qk = {h: _nn(qk16[h], qk16[h]) for h in heads}
                yield
                continue
            if span < INV_BASE:
                both = {h: _nn(qk16[h], _pair(qk16[h], dm[h].astype(BF16))) for h in heads}
                dm = {h: dm[h] + qk[h] + both[h][:, cs:] for h in heads}
                qk = {h: both[h][:, :cs] for h in heads}
            else:
                dm = {h: dm[h] + qk[h] + _nn(qk16[h], dm[h].astype(BF16)) for h in heads}
            yield
        for size, band in levels:
            idle = {h: take(dm[h], 1 - active, size) for h in heads}
            act = {h: take(dm[h], active, size) for h in heads}
            ao = {h: jnp.where(band, take(a_mat[h], active, size), 0.0) for h in heads}
            ao_full = {h: merge(jnp.zeros_like(ao[h]), ao[h], size).astype(BF16) for h in heads}
            ya = {h: ao[h] + _nn(act[h].astype(BF16), ao_full[h]) for h in heads}
            yield
            za = {h: ya[h] + _nn(ya[h].astype(BF16), dm[h].astype(BF16)) for h in heads}
            dm = {h: merge(idle[h], act[h] - za[h], size) for h in heads}
            yield
        egc = {h: jnp.exp(gcol[h]) for h in heads}
        uw = {}
        for h in heads:
            rhs = _pair(v_ref[0, tsl, h * hd:(h + 1) * hd] * bcol[h], kp[h // 2] * (bcol[h] * egc[h]))
            uw[h] = rhs + _nn(dm[h].astype(BF16), rhs.astype(BF16))
        yield
        assert len(updated) == position, "state read emitted before the previous chunk's update"
        s = {h: s_scr[h] for h in heads}
        s_bd = {p: _block_diag(s[2 * p].astype(BF16), s[2 * p + 1].astype(BF16)) for p in pairs}
        wq = {}
        for p in pairs:
            w_cat = _pair(uw[2 * p][:, hd:], uw[2 * p + 1][:, hd:]).astype(BF16)
            qd_cat = _pair(qp[p] * egc[2 * p], qp[p] * egc[2 * p + 1]).astype(BF16)
            wq[p] = _nn(jnp.concatenate([w_cat, qd_cat], axis=0), s_bd[p])
        yield
        vn16 = {p: (_pair(uw[2 * p][:, :hd], uw[2 * p + 1][:, :hd]) - wq[p][:cs]).astype(BF16) for p in pairs}
        for p in pairs:
            vn_bd = _block_diag(vn16[p][:, :hd], vn16[p][:, hd:])
            o_ref[0, tsl, 2 * p * hd:(2 * p + 2) * hd] = (wq[p][cs:] + _nn(qkd16[p], vn_bd)).astype(ACT)
        yield
        for h in heads:
            kt = kp[h // 2] * jnp.exp(gdiff[:, h:h + 1])
            gtot = jnp.exp(gcol[h][edge:edge + 1, :])
            vn_h = vn16[h // 2][:, (h % 2) * hd:(h % 2 + 1) * hd]
            s_scr[h] = s[h] * gtot + _tn(kt.astype(BF16), vn_h)
        updated.append(r0)

    starts = range(0, q_ref.shape[1], cs)
    running = [group_stages(0, r0, n) for n, r0 in enumerate(reversed(starts) if reverse else starts)]
    turn = 0
    while running:
        for n, stages in enumerate(list(running)):
            if turn >= n * DELTA_CHUNK_SKEW and next(stages, "done") == "done":
                running.remove(stages)
        turn += 1


def _dn_out_kernel(of_ref, or_ref, z_ref, h1_ref, nw_ref, wo_ref, gain_ref, out_ref, ybuf):
    width = z_ref.shape[-1]
    for c0 in range(0, width, DN_HEAD_DIM):
        c1 = c0 + DN_HEAD_DIM
        o = of_ref[0, :, c0:c1].astype(F32) + or_ref[0, :, c0:c1].astype(F32)
        z = z_ref[0, :, c0:c1].astype(F32)
        y = o * lax.rsqrt(jnp.mean(o * o, axis=-1, keepdims=True) + NORM_EPS) * nw_ref[...]
        ybuf[:, c0:c1] = (y * _silu(z)).astype(BF16)
    y = _nn(ybuf[...], wo_ref[...])
    out_ref[0] = h1_ref[0] + _rms(y, gain_ref[...])


def _params(sem):
    return pltpu.CompilerParams(dimension_semantics=sem, vmem_limit_bytes=VMEM_LIMIT)


def _const_spec(shape):
    nd = len(shape)
    return pl.BlockSpec(shape, lambda *_: (0,) * nd)


def _halo_specs(tt, nt, d_model):
    per = tt // HALO
    last = nt * per - 1
    prev = pl.BlockSpec((1, HALO, d_model), lambda b, i: (b, jnp.maximum(i * per - 1, 0), 0))
    main = pl.BlockSpec((1, tt, d_model), lambda b, i: (b, i, 0))
    nxt = pl.BlockSpec((1, HALO, d_model), lambda b, i: (b, jnp.minimum((i + 1) * per, last), 0))
    return prev, main, nxt


def kernel(x, meta_tokens, norm_w, rg_w_in, rg_conv_w, rg_conv_b, rg_gate_w, rg_gate_b, rg_lambda,
           rg_w_out, dn_w_in, dn_conv_w, dn_a_log, dn_dt_bias, dn_norm_w, dn_w_out):
    bsz, seq, d_model = x.shape
    n_meta = meta_tokens.shape[0]
    tt = TIME_TILE
    assert seq % tt == 0 and n_meta <= tt and tt % CHUNK == 0
    npad = tt - n_meta
    tf = tt + seq
    nt = tf // tt
    rg_width = rg_w_out.shape[1]
    v_width = dn_w_out.shape[1]
    nheads = dn_a_log.shape[-1]
    qk_width = (dn_conv_w.shape[-1] - v_width) // 2
    assert v_width == 2 * qk_width
    conv_ch = 2 * qk_width + v_width
    assert rg_width % (8 * LANES) == 0 and nheads * DN_HEAD_DIM == v_width and 2 * nheads <= LANES

    meta_tile = jnp.concatenate([jnp.zeros((npad, d_model), F32), meta_tokens.astype(F32)], axis=0)

    prev_spec, main_spec, next_spec = _halo_specs(tt, nt, d_model)
    per = tt // HALO
    x_prev = pl.BlockSpec((1, HALO, d_model), lambda b, i: (b, jnp.maximum((i - 1) * per - 1, 0), 0))
    x_main = pl.BlockSpec((1, tt, d_model), lambda b, i: (b, jnp.maximum(i - 1, 0), 0))
    x_next = pl.BlockSpec((1, HALO, d_model), lambda b, i: (b, jnp.minimum(i * per, (nt - 1) * per - 1), 0))
    tile = lambda w: pl.BlockSpec((1, tt, w), lambda b, i: (b, i, 0))
    rtile = lambda w: pl.BlockSpec((1, tt, w), lambda b, i: (b, nt - 1 - i, 0))

    w_in = rg_w_in[0].astype(BF16)
    xb, zb = pl.pallas_call(
        functools.partial(_rg_in_kernel, tt=tt, nt=nt, npad=npad),
        grid=(bsz, nt),
        in_specs=[x_prev, x_main, x_next, _const_spec((tt, d_model)), _const_spec((1, d_model)),
                  _const_spec((d_model, rg_width)), _const_spec((d_model, rg_width)),
                  _const_spec((CONV_WIDTH, rg_width)), _const_spec((1, rg_width))],
        out_specs=[tile(rg_width), tile(rg_width)],
        out_shape=[jax.ShapeDtypeStruct((bsz, tf, rg_width), ACT)] * 2,
        scratch_shapes=[pltpu.VMEM((tt + 2 * HALO, d_model), BF16),
                        pltpu.VMEM((rg_width // LANES, PRE_PITCH * (tt + 2 * HALO), LANES), F32)],
        compiler_params=_params(("parallel", "arbitrary")),
        name="rg_in",
    )(x, x, x, meta_tile, norm_w[0, 0][None], w_in[:, :rg_width], w_in[:, rg_width:],
      rg_conv_w[0], rg_conv_b[0][None])

    pitch = tt + SCAN_SLAB_PAD
    nslab = rg_width // LANES
    gate_w = (0.5 * rg_gate_w[0]).astype(BF16)
    gate_b = 0.5 * rg_gate_b[0].reshape(2, 2, 1, rg_width)
    slab = pltpu.VMEM((nslab * pitch, LANES), F32)
    hf, hr = pl.pallas_call(
        functools.partial(_rg_scan_kernel, tt=tt, pitch=pitch),
        grid=(bsz, nt),
        in_specs=[tile(rg_width), rtile(rg_width), _const_spec(gate_w.shape),
                  _const_spec(gate_b.shape), _const_spec((2, 1, rg_width))],
        out_specs=[tile(rg_width), rtile(rg_width)],
        out_shape=[jax.ShapeDtypeStruct((bsz, tf, rg_width), ACT)] * 2,
        scratch_shapes=[slab] * 6 + [pltpu.VMEM((2 * nslab // 8, 8, LANES), F32)],
        compiler_params=_params(("parallel", "arbitrary")),
        name="rg_scan",
    )(xb, xb, gate_w, gate_b, rg_lambda[0][:, None, :])

    h1 = pl.pallas_call(
        _rg_out_kernel,
        grid=(bsz, nt),
        in_specs=[tile(rg_width), tile(rg_width), tile(rg_width), x_main, _const_spec((tt, d_model)),
                  _const_spec((rg_width, d_model)), _const_spec((1, d_model))],
        out_specs=tile(d_model),
        out_shape=jax.ShapeDtypeStruct((bsz, tf, d_model), F32),
        compiler_params=_params(("parallel", "parallel")),
        name="rg_out",
    )(hf, hr, zb, x, meta_tile, rg_w_out[0].astype(BF16), norm_w[0, 1][None])

    w_in = dn_w_in[0]
    w_ab = w_in[:, conv_ch + v_width:]
    w_a = w_ab[:, :2 * nheads].reshape(d_model, 2, nheads)
    w_b = w_ab[:, 2 * nheads:].reshape(d_model, 2, nheads)
    w_ab_nat = jnp.concatenate(
        [w_a, w_b, jnp.zeros((d_model, 2, LANES - 2 * nheads), F32)], axis=-1)
    w_ab_nat = jnp.moveaxis(w_ab_nat, 1, 0).astype(BF16)
    w_ab_t = w_ab[:, :2 * nheads].T.astype(BF16)
    lane_pad = lambda p: jnp.pad(p, ((0, 0), (0, LANES - nheads)))[:, None, :]
    ti = jnp.arange(tt)
    lmat = ((ti[:, None] >= ti[None, :]) & (ti[:, None] // CHUNK == ti[None, :] // CHUNK)).astype(BF16)
    umat = lmat.T

    q, k, v, z, gb, gct = pl.pallas_call(
        functools.partial(_dn_in_kernel, tt=tt, nt=nt, npad=npad, qk_width=qk_width),
        grid=(bsz, nt),
        in_specs=[prev_spec, main_spec, next_spec, _const_spec((1, d_model)),
                  _const_spec((d_model, conv_ch)), _const_spec((d_model, v_width)),
                  _const_spec((2, d_model, LANES)), _const_spec((2 * nheads, d_model)),
                  _const_spec((CONV_WIDTH, conv_ch)),
                  _const_spec((2, 1, LANES)), _const_spec((2, 1, LANES)),
                  _const_spec((2 * nheads, 1)), _const_spec((2 * nheads, 1)),
                  _const_spec((tt, tt)), _const_spec((tt, tt))],
        out_specs=[tile(qk_width), tile(qk_width), tile(v_width), tile(v_width),
                   pl.BlockSpec((2, 1, tt, LANES), lambda b, i: (0, b, i, 0)),
                   pl.BlockSpec((1, 2 * nheads, tt), lambda b, i: (b, 0, i))],
        out_shape=[jax.ShapeDtypeStruct((bsz, tf, qk_width), ACT),
                   jax.ShapeDtypeStruct((bsz, tf, qk_width), ACT),
                   jax.ShapeDtypeStruct((bsz, tf, v_width), ACT),
                   jax.ShapeDtypeStruct((bsz, tf, v_width), ACT),
                   jax.ShapeDtypeStruct((2, bsz, tf, LANES), F32),
                   jax.ShapeDtypeStruct((bsz, 2 * nheads, tf), F32)],
        scratch_shapes=[pltpu.VMEM((tt + 2 * HALO, d_model), BF16),
                        pltpu.VMEM((conv_ch // LANES, PRE_PITCH * (tt + 2 * HALO), LANES), F32)],
        compiler_params=_params(("parallel", "arbitrary")),
        name="dn_in",
    )(h1, h1, h1, norm_w[1, 0][None], w_in[:, :conv_ch].astype(BF16),
      w_in[:, conv_ch:conv_ch + v_width].astype(BF16), w_ab_nat, w_ab_t, dn_conv_w[0],
      lane_pad(dn_a_log[0]), lane_pad(dn_dt_bias[0]),
      dn_a_log[0].reshape(2 * nheads, 1), dn_dt_bias[0].reshape(2 * nheads, 1), lmat, umat)

    nch = tf // DELTA_BLOCK

    def delta_direction(reverse):
        d = int(reverse)
        cidx = (lambda c: nch - 1 - c) if reverse else (lambda c: c)
        ctile = lambda w: pl.BlockSpec((1, DELTA_BLOCK, w), lambda b, c: (b, cidx(c), 0))
        return pl.pallas_call(
            functools.partial(_delta_kernel, nheads=nheads, reverse=reverse),
            grid=(bsz, nch),
            in_specs=[ctile(qk_width), ctile(qk_width), ctile(v_width),
                      pl.BlockSpec((1, 1, DELTA_BLOCK, LANES), lambda b, c: (d, b, cidx(c), 0)),
                      pl.BlockSpec((1, nheads, DELTA_BLOCK), lambda b, c: (b, d, cidx(c)))],
            out_specs=ctile(v_width),
            out_shape=jax.ShapeDtypeStruct((bsz, tf, v_width), ACT),
            scratch_shapes=[pltpu.VMEM((nheads, DN_HEAD_DIM, DN_HEAD_DIM), F32)],
            compiler_params=_params(("parallel", "arbitrary")),
            name="delta_rev" if reverse else "delta_fwd",
        )(q, k, v, gb, gct)

    o_fwd = delta_direction(False)
    o_rev = delta_direction(True)

    xtile = lambda w: pl.BlockSpec((1, tt, w), lambda b, i: (b, i + 1, 0))
    out = pl.pallas_call(
        _dn_out_kernel,
        grid=(bsz, nt - 1),
        in_specs=[xtile(v_width), xtile(v_width), xtile(v_width), xtile(d_model), _const_spec((1, DN_HEAD_DIM)),
                  _const_spec((v_width, d_model)), _const_spec((1, d_model))],
        out_specs=pl.BlockSpec((1, tt, d_model), lambda b, i: (b, i, 0)),
        out_shape=jax.ShapeDtypeStruct((bsz, seq, d_model), F32),
        scratch_shapes=[pltpu.VMEM((tt, v_width), BF16)],
        compiler_params=_params(("parallel", "parallel")),
        name="dn_out",
    )(o_fwd, o_rev, z, h1, dn_norm_w[0][None], dn_w_out[0].astype(BF16), norm_w[1, 1][None])
    return out
```

```python
import functools

import jax
import jax.numpy as jnp
from jax import lax
from jax.experimental import pallas as pl
from jax.experimental.pallas import tpu as pltpu

F32 = jnp.float32
BF16 = jnp.bfloat16
ACT = jnp.bfloat16

NORM_EPS = 1e-6
RG_C = 8.0
RG_BLOCK = 256
DN_HEAD_DIM = 128
CONV_WIDTH = 4

TIME_TILE = 256
HALO = 16
CHUNK = 128
LANES = 128
SCAN_SLAB_PAD = 4
RG_PROJ_BLOCK = 256
DN_PROJ_BLOCK = 1024
CONV_ROWS = 64
PRE_PITCH = 2
SCAN_UNROLL = 16
F32_TINY = 1.1754944e-38
DELTA_HEAD_GROUP = 16
DELTA_BLOCK = 384
DELTA_CHUNK_SKEW = 5
INV_BASE = 8
VMEM_LIMIT = 56 * 1024 * 1024


def _nn(a, b):
    return jnp.dot(a, b, preferred_element_type=F32)


def _nt(a, b):
    return lax.dot_general(a, b, (((1,), (1,)), ((), ())), preferred_element_type=F32)


def _tn(a, b):
    return lax.dot_general(a, b, (((0,), (0,)), ((), ())), preferred_element_type=F32)


def _pair(a, b):
    return jnp.concatenate([a, b], axis=1)


def _block_diag(a, b):
    za = jnp.zeros((a.shape[0], b.shape[1]), a.dtype)
    zb = jnp.zeros((b.shape[0], a.shape[1]), a.dtype)
    return jnp.concatenate([_pair(a, za), _pair(zb, b)], axis=0)


def _sigmoid(x):
    return 0.5 * jnp.tanh(0.5 * x) + 0.5


def _silu(x):
    h = 0.5 * x
    return h * jnp.tanh(h) + h


def _softplus(x):
    return jnp.maximum(x, 0.0) + jnp.log1p(jnp.exp(-jnp.abs(x)))


def _rms(h, gain):
    return h * lax.rsqrt(jnp.mean(h * h, axis=-1, keepdims=True) + NORM_EPS) * gain


def _split_hi_lo(x):
    hi = x.astype(BF16)
    lo = (x - hi.astype(F32)).astype(BF16)
    return hi, lo


def _fill_ybuf(hp, hm, hn, gain, ybuf, i, nt, tt):
    ym = _rms(hm, gain).astype(BF16)
    yp = jnp.where(i > 0, _rms(hp, gain), 0.0).astype(BF16)
    yn = jnp.where(i < nt - 1, _rms(hn, gain), 0.0).astype(BF16)
    ybuf[0:HALO, :] = yp
    ybuf[HALO:HALO + tt, :] = ym
    ybuf[HALO + tt:, :] = yn
    return ym


def _store_pre(pre, res, c0):
    rows = res.shape[0]
    for s in range(res.shape[1] // LANES):
        pre[c0 // LANES + s, pl.ds(0, rows, stride=PRE_PITCH), :] = res[:, s * LANES:(s + 1) * LANES]


def _conv_tile(pre, cw_ref, r0, nrows, slab):
    acc = None
    for k in range(CONV_WIDTH):
        start = PRE_PITCH * (HALO - 1 + k + r0)
        tap = pre[slab, pl.ds(start, nrows, stride=PRE_PITCH), :]
        term = cw_ref[k:k + 1, slab * LANES:(slab + 1) * LANES] * tap
        acc = term if acc is None else acc + term
    return acc


def _rg_in_kernel(xp_ref, xm_ref, xn_ref, meta_ref, gain_ref, wx_ref, wz_ref, cw_ref, cb_ref,
                  xb_ref, zb_ref, ybuf, pre, *, tt, nt, npad):
    i = pl.program_id(1)
    hm = jnp.where(i == 0, meta_ref[...], xm_ref[0])
    hp = jnp.where(i == 1, meta_ref[tt - HALO:, :], xp_ref[0])
    ym = _fill_ybuf(hp, hm, xn_ref[0], gain_ref[...], ybuf, i, nt, tt)
    width = xb_ref.shape[-1]
    PROJ_BLOCK = RG_PROJ_BLOCK
    nblk = width // PROJ_BLOCK
    zblk = zb_ref.shape[-1] // nblk
    for j in range(nblk + 1):
        if j < nblk:
            c0 = j * PROJ_BLOCK
            _store_pre(pre, _nn(ybuf[...], wx_ref[:, c0:c0 + PROJ_BLOCK]), c0)
            zb_ref[0, :, j * zblk:(j + 1) * zblk] = _nn(ym, wz_ref[:, j * zblk:(j + 1) * zblk]).astype(ACT)
        if j > 0:
            for slab in range((j - 1) * PROJ_BLOCK // LANES, j * PROJ_BLOCK // LANES):
                lanes = slice(slab * LANES, (slab + 1) * LANES)
                for r0 in range(0, tt, CONV_ROWS):
                    xb_ref[0, r0:r0 + CONV_ROWS, lanes] = (
                        _conv_tile(pre, cw_ref, r0, CONV_ROWS, slab) + cb_ref[:, lanes]).astype(ACT)

    @pl.when(i == 0)
    def _():
        xb_ref[0, :npad, :] = jnp.zeros((npad, width), ACT)


def _rg_scan_kernel(xf_ref, xr_ref, gw_ref, gb_ref, lam_ref, hf_ref, hr_ref,
                    af_scr, bf_scr, ar_scr, br_scr, of_scr, or_scr, h_scr, *, tt, pitch):
    i = pl.program_id(1)

    @pl.when(i == 0)
    def _():
        h_scr[...] = jnp.zeros_like(h_scr)

    width = xf_ref.shape[-1]
    nblk = width // RG_BLOCK
    for d, (x_ref, a_scr, b_scr) in enumerate(((xf_ref, af_scr, bf_scr), (xr_ref, ar_scr, br_scr))):
        for n in range(nblk):
            c0 = n * RG_BLOCK
            xb16 = x_ref[0, :, c0:c0 + RG_BLOCK].astype(BF16)
            x = xb16.astype(F32)
            tr = jnp.tanh(_nn(xb16, gw_ref[d, 0, n]) + gb_ref[d, 0, :, c0:c0 + RG_BLOCK])
            ti = jnp.tanh(_nn(xb16, gw_ref[d, 1, n]) + gb_ref[d, 1, :, c0:c0 + RG_BLOCK])
            kk = (-0.5 * RG_C) * _softplus(-lam_ref[d, :, c0:c0 + RG_BLOCK])
            log_a = kk * tr + kk
            a = jnp.exp(log_a)
            y = jnp.tanh(log_a) * (-1.0 - a * a)
            mult = y * lax.rsqrt(jnp.maximum(y, F32_TINY))
            b = (mult * x) * (0.5 * ti + 0.5)
            for s in range(RG_BLOCK // LANES):
                q = n * (RG_BLOCK // LANES) + s
                a_scr[q * pitch:q * pitch + tt, :] = a[:, s * LANES:(s + 1) * LANES]
                b_scr[q * pitch:q * pitch + tt, :] = b[:, s * LANES:(s + 1) * LANES]

    nslab = width // LANES
    ngrp = nslab // 8

    def four_steps(a_scr, b_scr, o_scr, base, times, h):
        idx = [pl.ds(base + t, 8, stride=pitch) for t in times]
        a = [a_scr[i, :] for i in idx]
        b = [b_scr[i, :] for i in idx]
        a01, b01 = a[1] * a[0], a[1] * b[0] + b[1]
        a23, b23 = a[3] * a[2], a[3] * b[2] + b[3]
        h2 = a01 * h + b01
        o_scr[idx[0], :] = a[0] * h + b[0]
        o_scr[idx[1], :] = h2
        o_scr[idx[2], :] = a[2] * h2 + b[2]
        h4 = (a23 * a01) * h + (a23 * b01 + b23)
        o_scr[idx[3], :] = h4
        return h4

    def body(j, carry):
        t = 4 * j
        new = [four_steps(af_scr, bf_scr, of_scr, g * 8 * pitch, [t + s for s in range(4)], carry[g])
               for g in range(ngrp)]
        new += [four_steps(ar_scr, br_scr, or_scr, g * 8 * pitch, [tt - 1 - t - s for s in range(4)],
                           carry[ngrp + g]) for g in range(ngrp)]
        return tuple(new)

    init = tuple(h_scr[j] for j in range(2 * ngrp))
    final = lax.fori_loop(0, tt // 4, body, init, unroll=SCAN_UNROLL // 4)
    for j in range(2 * ngrp):
        h_scr[j] = final[j]
    for q in range(nslab):
        hf_ref[0, :, q * LANES:(q + 1) * LANES] = of_scr[q * pitch:q * pitch + tt, :].astype(ACT)
        hr_ref[0, :, q * LANES:(q + 1) * LANES] = or_scr[q * pitch:q * pitch + tt, :].astype(ACT)


def _rg_out_kernel(hf_ref, hr_ref, z_ref, x_ref, meta_ref, wo_ref, gain_ref, out_ref):
    h0 = jnp.where(pl.program_id(1) == 0, meta_ref[...], x_ref[0])
    z = z_ref[0].astype(F32)
    o = (hf_ref[0].astype(F32) + hr_ref[0].astype(F32)) * _silu(z)
    y = _nn(o.astype(BF16), wo_ref[...])
    out_ref[0] = h0 + _rms(y, gain_ref[...])


def _dn_in_kernel(hp_ref, hm_ref, hn_ref, gain_ref, wqkv_ref, wz_ref, wab_ref, wabt_ref, cw_ref,
                  alog_row_ref, dtb_row_ref, alog_col_ref, dtb_col_ref, lmat_ref, umat_ref,
                  q_ref, k_ref, v_ref, z_ref, gb_ref, gct_ref, ybuf, pre, *, tt, nt, npad, qk_width):
    i = pl.program_id(1)
    ym = _fill_ybuf(hp_ref[0], hm_ref[0], hn_ref[0], gain_ref[...], ybuf, i, nt, tt)
    row = i * tt + lax.broadcasted_iota(jnp.int32, (tt, 1), 0)
    valid = row >= npad
    scale = DN_HEAD_DIM ** -0.5
    v_width = v_ref.shape[-1]
    PROJ_BLOCK = DN_PROJ_BLOCK
    nblk = (2 * qk_width + v_width) // PROJ_BLOCK
    zblk = z_ref.shape[-1] // nblk

    lane = lax.broadcasted_iota(jnp.int32, (1, LANES), 1)
    nheads = gct_ref.shape[1] // 2
    for d in range(2):
        ab = _nn(ym, wab_ref[d])
        g = -jnp.exp(alog_row_ref[d]) * _softplus(ab + dtb_row_ref[d])
        g = jnp.where(valid, g, 0.0)
        g_hi, g_lo = _split_hi_lo(g)
        cmat = lmat_ref[...] if d == 0 else umat_ref[...]
        gc = _nn(cmat, g_hi) + _nn(cmat, g_lo)
        beta = jnp.where(valid, _sigmoid(ab), 0.0)
        gb_ref[d, 0] = jnp.where(lane < nheads, gc, beta)

    col = i * tt + lax.broadcasted_iota(jnp.int32, (1, tt), 1)
    abt = _nt(wabt_ref[...], ym)
    gt = -jnp.exp(alog_col_ref[...]) * _softplus(abt + dtb_col_ref[...])
    gt = jnp.where(col >= npad, gt, 0.0)
    gt_hi, gt_lo = _split_hi_lo(gt)
    gct_f = _nn(gt_hi[:nheads], umat_ref[...]) + _nn(gt_lo[:nheads], umat_ref[...])
    gct_r = _nn(gt_hi[nheads:], lmat_ref[...]) + _nn(gt_lo[nheads:], lmat_ref[...])
    gct_ref[0, :nheads, :] = gct_f
    gct_ref[0, nheads:, :] = gct_r

    def consume(c0):
        if c0 < 2 * qk_width:
            for h0 in range(c0, c0 + PROJ_BLOCK, DN_HEAD_DIM):
                for r0 in range(0, tt, CONV_ROWS):
                    acc = _silu(_conv_tile(pre, cw_ref, r0, CONV_ROWS, h0 // LANES))
                    inv = lax.rsqrt(jnp.sum(acc * acc, axis=-1, keepdims=True) + NORM_EPS)
                    if h0 < qk_width:
                        q_ref[0, r0:r0 + CONV_ROWS, h0:h0 + DN_HEAD_DIM] = (acc * (inv * scale)).astype(ACT)
                    else:
                        k0 = h0 - qk_width
                        k_ref[0, r0:r0 + CONV_ROWS, k0:k0 + DN_HEAD_DIM] = (acc * inv).astype(ACT)
        else:
            for slab in range(c0 // LANES, (c0 + PROJ_BLOCK) // LANES):
                v0 = slab * LANES - 2 * qk_width
                for r0 in range(0, tt, CONV_ROWS):
                    v_ref[0, r0:r0 + CONV_ROWS, v0:v0 + LANES] = _silu(
                        _conv_tile(pre, cw_ref, r0, CONV_ROWS, slab)).astype(ACT)

    for j in range(nblk + 1):
        if j < nblk:
            c0 = j * PROJ_BLOCK
            _store_pre(pre, _nn(ybuf[...], wqkv_ref[:, c0:c0 + PROJ_BLOCK]), c0)
            z_ref[0, :, j * zblk:(j + 1) * zblk] = _nn(ym, wz_ref[:, j * zblk:(j + 1) * zblk]).astype(ACT)
        if j > 0:
            consume((j - 1) * PROJ_BLOCK)

    @pl.when(i == 0)
    def _():
        q_ref[0, :npad, :] = jnp.zeros((npad, qk_width), ACT)
        k_ref[0, :npad, :] = jnp.zeros((npad, qk_width), ACT)
        v_ref[0, :npad, :] = jnp.zeros((npad, v_width), ACT)


def _delta_kernel(q_ref, k_ref, v_ref, gb_ref, gct_ref, o_ref, s_scr, *, nheads, reverse):
    c = pl.program_id(1)

    @pl.when(c == 0)
    def _():
        s_scr[...] = jnp.zeros_like(s_scr)

    cs = CHUNK
    rows = lax.broadcasted_iota(jnp.int32, (cs, cs), 0)
    cols = lax.broadcasted_iota(jnp.int32, (cs, cs), 1)
    incl = rows <= cols if reverse else rows >= cols
    strict = rows < cols if reverse else rows > cols
    base_mask = (rows ^ cols) < INV_BASE
    active = 0 if reverse else 1
    half_rows = lax.broadcasted_iota(jnp.int32, (cs // 2, cs), 0)
    half_cols = lax.broadcasted_iota(jnp.int32, (cs // 2, cs), 1)
    levels = []
    size = INV_BASE
    while size < cs:
        shift = size.bit_length() - 1
        levels.append((size, (half_cols >> shift) == 2 * (half_rows >> shift) + (1 - active)))
        size *= 2

    def take(x, half, size):
        return jnp.concatenate([x[b0 + half * size:b0 + (half + 1) * size] for b0 in range(0, cs, 2 * size)], axis=0)

    def merge(idle, act, size):
        pieces = []
        for n in range(cs // (2 * size)):
            pair = (idle[n * size:(n + 1) * size], act[n * size:(n + 1) * size])
            pieces += pair if active == 1 else pair[::-1]
        return jnp.concatenate(pieces, axis=0)

    edge = 0 if reverse else cs - 1

    hd = DN_HEAD_DIM
    updated = []

    def group_stages(g0, r0, position):
        tsl = slice(r0, r0 + cs)
        gb = gb_ref[0, 0, tsl, :]
        gct = gct_ref[0, :, tsl]
        gdiff = gb[edge:edge + 1, :] - gb
        heads = list(range(g0, g0 + DELTA_HEAD_GROUP))
        pairs = list(range(g0 // 2, (g0 + DELTA_HEAD_GROUP) // 2))
        qp = {p: q_ref[0, tsl, p * hd:(p + 1) * hd] for p in pairs}
        kp = {p: k_ref[0, tsl, p * hd:(p + 1) * hd] for p in pairs}
        kp16 = {p: kp[p].astype(BF16) for p in pairs}
        kk = {p: _nt(kp16[p], kp16[p]) for p in pairs}
        qk0 = {p: _nt(qp[p].astype(BF16), kp16[p]) for p in pairs}
        gcol = {h: gb[:, h:h + 1] for h in heads}
        bcol = {h: gb[:, nheads + h:nheads + h + 1] for h in heads}
        decay = {h: jnp.exp(jnp.where(incl, gcol[h] - gct[h:h + 1, :], -1e30)) for h in heads}
        a_mat = {h: jnp.where(strict, kk[h // 2] * decay[h], 0.0) * bcol[h] for h in heads}
        qkd16 = {p: _pair(*[(qk0[p] * decay[h]).astype(BF16) for h in (2 * p, 2 * p + 1)]) for p in pairs}
        yield
        dm = {h: -jnp.where(base_mask, a_mat[h], 0.0) for h in heads}
        qk = dm
        span = 1
        while span < INV_BASE:
            qk16 = {h: qk[h].astype(BF16) for h in heads}
            span *= 2
            if span == 2:
                qk = {h: _nn(qk16[h], qk16[h]) for h in heads}
                yield
                continue
            if span < INV_BASE:
                both = {h: _nn(qk16[h], _pair(qk16[h], dm[h].astype(BF16))) for h in heads}
                dm = {h: dm[h] + qk[h] + both[h][:, cs:] for h in heads}
                qk = {h: both[h][:, :cs] for h in heads}
            else:
                dm = {h: dm[h] + qk[h] + _nn(qk16[h], dm[h].astype(BF16)) for h in heads}
            yield
        for size, band in levels:
            idle = {h: take(dm[h], 1 - active, size) for h in heads}
            act = {h: take(dm[h], active, size) for h in heads}
            ao = {h: jnp.where(band, take(a_mat[h], active, size), 0.0) for h in heads}
            ao_full = {h: merge(jnp.zeros_like(ao[h]), ao[h], size).astype(BF16) for h in heads}
            ya = {h: ao[h] + _nn(act[h].astype(BF16), ao_full[h]) for h in heads}
            yield
            za = {h: ya[h] + _nn(ya[h].astype(BF16), dm[h].astype(BF16)) for h in heads}
            dm = {h: merge(idle[h], act[h] - za[h], size) for h in heads}
            yield
        egc = {h: jnp.exp(gcol[h]) for h in heads}
        uw = {}
        for h in heads:
            rhs = _pair(v_ref[0, tsl, h * hd:(h + 1) * hd] * bcol[h], kp[h // 2] * (bcol[h] * egc[h]))
            uw[h] = rhs + _nn(dm[h].astype(BF16), rhs.astype(BF16))
        yield
        assert len(updated) == position, "state read emitted before the previous chunk's update"
        s = {h: s_scr[h] for h in heads}
        s_bd = {p: _block_diag(s[2 * p].astype(BF16), s[2 * p + 1].astype(BF16)) for p in pairs}
        wq = {}
        for p in pairs:
            w_cat = _pair(uw[2 * p][:, hd:], uw[2 * p + 1][:, hd:]).astype(BF16)
            qd_cat = _pair(qp[p] * egc[2 * p], qp[p] * egc[2 * p + 1]).astype(BF16)
            wq[p] = _nn(jnp.concatenate([w_cat, qd_cat], axis=0), s_bd[p])
        yield
        vn16 = {p: (_pair(uw[2 * p][:, :hd], uw[2 * p + 1][:, :hd]) - wq[p][:cs]).astype(BF16) for p in pairs}
        for p in pairs:
            vn_bd = _block_diag(vn16[p][:, :hd], vn16[p][:, hd:])
            o_ref[0, tsl, 2 * p * hd:(2 * p + 2) * hd] = (wq[p][cs:] + _nn(qkd16[p], vn_bd)).astype(ACT)
        yield
        for h in heads:
            kt = kp[h // 2] * jnp.exp(gdiff[:, h:h + 1])
            gtot = jnp.exp(gcol[h][edge:edge + 1, :])
            vn_h = vn16[h // 2][:, (h % 2) * hd:(h % 2 + 1) * hd]
            s_scr[h] = s[h] * gtot + _tn(kt.astype(BF16), vn_h)
        updated.append(r0)

    starts = range(0, q_ref.shape[1], cs)
    running = [group_stages(0, r0, n) for n, r0 in enumerate(reversed(starts) if reverse else starts)]
    turn = 0
    while running:
        for n, stages in enumerate(list(running)):
            if turn >= n * DELTA_CHUNK_SKEW and next(stages, "done") == "done":
                running.remove(stages)
        turn += 1


def _dn_out_kernel(of_ref, or_ref, z_ref, h1_ref, nw_ref, wo_ref, gain_ref, out_ref, ybuf):
    width = z_ref.shape[-1]
    for c0 in range(0, width, DN_HEAD_DIM):
        c1 = c0 + DN_HEAD_DIM
        o = of_ref[0, :, c0:c1].astype(F32) + or_ref[0, :, c0:c1].astype(F32)
        z = z_ref[0, :, c0:c1].astype(F32)
        y = o * lax.rsqrt(jnp.mean(o * o, axis=-1, keepdims=True) + NORM_EPS) * nw_ref[...]
        ybuf[:, c0:c1] = (y * _silu(z)).astype(BF16)
    y = _nn(ybuf[...], wo_ref[...])
    out_ref[0] = h1_ref[0] + _rms(y, gain_ref[...])


def _params(sem):
    return pltpu.CompilerParams(dimension_semantics=sem, vmem_limit_bytes=VMEM_LIMIT)


def _const_spec(shape):
    nd = len(shape)
    return pl.BlockSpec(shape, lambda *_: (0,) * nd)


def _halo_specs(tt, nt, d_model):
    per = tt // HALO
    last = nt * per - 1
    prev = pl.BlockSpec((1, HALO, d_model), lambda b, i: (b, jnp.maximum(i * per - 1, 0), 0))
    main = pl.BlockSpec((1, tt, d_model), lambda b, i: (b, i, 0))
    nxt = pl.BlockSpec((1, HALO, d_model), lambda b, i: (b, jnp.minimum((i + 1) * per, last), 0))
    return prev, main, nxt


def kernel(x, meta_tokens, norm_w, rg_w_in, rg_conv_w, rg_conv_b, rg_gate_w, rg_gate_b, rg_lambda,
           rg_w_out, dn_w_in, dn_conv_w, dn_a_log, dn_dt_bias, dn_norm_w, dn_w_out):
    bsz, seq, d_model = x.shape
    n_meta = meta_tokens.shape[0]
    tt = TIME_TILE
    assert seq % tt == 0 and n_meta <= tt and tt % CHUNK == 0
    npad = tt - n_meta
    tf = tt + seq
    nt = tf // tt
    rg_width = rg_w_out.shape[1]
    v_width = dn_w_out.shape[1]
    nheads = dn_a_log.shape[-1]
    qk_width = (dn_conv_w.shape[-1] - v_width) // 2
    assert v_width == 2 * qk_width
    conv_ch = 2 * qk_width + v_width
    assert rg_width % (8 * LANES) == 0 and nheads * DN_HEAD_DIM == v_width and 2 * nheads <= LANES

    meta_tile = jnp.concatenate([jnp.zeros((npad, d_model), F32), meta_tokens.astype(F32)], axis=0)

    prev_spec, main_spec, next_spec = _halo_specs(tt, nt, d_model)
    per = tt // HALO
    x_prev = pl.BlockSpec((1, HALO, d_model), lambda b, i: (b, jnp.maximum((i - 1) * per - 1, 0), 0))
    x_main = pl.BlockSpec((1, tt, d_model), lambda b, i: (b, jnp.maximum(i - 1, 0), 0))
    x_next = pl.BlockSpec((1, HALO, d_model), lambda b, i: (b, jnp.minimum(i * per, (nt - 1) * per - 1), 0))
    tile = lambda w: pl.BlockSpec((1, tt, w), lambda b, i: (b, i, 0))
    rtile = lambda w: pl.BlockSpec((1, tt, w), lambda b, i: (b, nt - 1 - i, 0))

    w_in = rg_w_in[0].astype(BF16)
    xb, zb = pl.pallas_call(
        functools.partial(_rg_in_kernel, tt=tt, nt=nt, npad=npad),
        grid=(bsz, nt),
        in_specs=[x_prev, x_main, x_next, _const_spec((tt, d_model)), _const_spec((1, d_model)),
                  _const_spec((d_model, rg_width)), _const_spec((d_model, rg_width)),
                  _const_spec((CONV_WIDTH, rg_width)), _const_spec((1, rg_width))],
        out_specs=[tile(rg_width), tile(rg_width)],
        out_shape=[jax.ShapeDtypeStruct((bsz, tf, rg_width), ACT)] * 2,
        scratch_shapes=[pltpu.VMEM((tt + 2 * HALO, d_model), BF16),
                        pltpu.VMEM((rg_width // LANES, PRE_PITCH * (tt + 2 * HALO), LANES), F32)],
        compiler_params=_params(("parallel", "arbitrary")),
        name="rg_in",
    )(x, x, x, meta_tile, norm_w[0, 0][None], w_in[:, :rg_width], w_in[:, rg_width:],
      rg_conv_w[0], rg_conv_b[0][None])

    pitch = tt + SCAN_SLAB_PAD
    nslab = rg_width // LANES
    gate_w = (0.5 * rg_gate_w[0]).astype(BF16)
    gate_b = 0.5 * rg_gate_b[0].reshape(2, 2, 1, rg_width)
    slab = pltpu.VMEM((nslab * pitch, LANES), F32)
    hf, hr = pl.pallas_call(
        functools.partial(_rg_scan_kernel, tt=tt, pitch=pitch),
        grid=(bsz, nt),
        in_specs=[tile(rg_width), rtile(rg_width), _const_spec(gate_w.shape),
                  _const_spec(gate_b.shape), _const_spec((2, 1, rg_width))],
        out_specs=[tile(rg_width), rtile(rg_width)],
        out_shape=[jax.ShapeDtypeStruct((bsz, tf, rg_width), ACT)] * 2,
        scratch_shapes=[slab] * 6 + [pltpu.VMEM((2 * nslab // 8, 8, LANES), F32)],
        compiler_params=_params(("parallel", "arbitrary")),
        name="rg_scan",
    )(xb, xb, gate_w, gate_b, rg_lambda[0][:, None, :])

    h1 = pl.pallas_call(
        _rg_out_kernel,
        grid=(bsz, nt),
        in_specs=[tile(rg_width), tile(rg_width), tile(rg_width), x_main, _const_spec((tt, d_model)),
                  _const_spec((rg_width, d_model)), _const_spec((1, d_model))],
        out_specs=tile(d_model),
        out_shape=jax.ShapeDtypeStruct((bsz, tf, d_model), F32),
        compiler_params=_params(("parallel", "parallel")),
        name="rg_out",
    )(hf, hr, zb, x, meta_tile, rg_w_out[0].astype(BF16), norm_w[0, 1][None])

    w_in = dn_w_in[0]
    w_ab = w_in[:, conv_ch + v_width:]
    w_a = w_ab[:, :2 * nheads].reshape(d_model, 2, nheads)
    w_b = w_ab[:, 2 * nheads:].reshape(d_model, 2, nheads)
    w_ab_nat = jnp.concatenate(
        [w_a, w_b, jnp.zeros((d_model, 2, LANES - 2 * nheads), F32)], axis=-1)
    w_ab_nat = jnp.moveaxis(w_ab_nat, 1, 0).astype(BF16)
    w_ab_t = w_ab[:, :2 * nheads].T.astype(BF16)
    lane_pad = lambda p: jnp.pad(p, ((0, 0), (0, LANES - nheads)))[:, None, :]
    ti = jnp.arange(tt)
    lmat = ((ti[:, None] >= ti[None, :]) & (ti[:, None] // CHUNK == ti[None, :] // CHUNK)).astype(BF16)
    umat = lmat.T

    q, k, v, z, gb, gct = pl.pallas_call(
        functools.partial(_dn_in_kernel, tt=tt, nt=nt, npad=npad, qk_width=qk_width),
        grid=(bsz, nt),
        in_specs=[prev_spec, main_spec, next_spec, _const_spec((1, d_model)),
                  _const_spec((d_model, conv_ch)), _const_spec((d_model, v_width)),
                  _const_spec((2, d_model, LANES)), _const_spec((2 * nheads, d_model)),
                  _const_spec((CONV_WIDTH, conv_ch)),
                  _const_spec((2, 1, LANES)), _const_spec((2, 1, LANES)),
                  _const_spec((2 * nheads, 1)), _const_spec((2 * nheads, 1)),
                  _const_spec((tt, tt)), _const_spec((tt, tt))],
        out_specs=[tile(qk_width), tile(qk_width), tile(v_width), tile(v_width),
                   pl.BlockSpec((2, 1, tt, LANES), lambda b, i: (0, b, i, 0)),
                   pl.BlockSpec((1, 2 * nheads, tt), lambda b, i: (b, 0, i))],
        out_shape=[jax.ShapeDtypeStruct((bsz, tf, qk_width), ACT),
                   jax.ShapeDtypeStruct((bsz, tf, qk_width), ACT),
                   jax.ShapeDtypeStruct((bsz, tf, v_width), ACT),
                   jax.ShapeDtypeStruct((bsz, tf, v_width), ACT),
                   jax.ShapeDtypeStruct((2, bsz, tf, LANES), F32),
                   jax.ShapeDtypeStruct((bsz, 2 * nheads, tf), F32)],
        scratch_shapes=[pltpu.VMEM((tt + 2 * HALO, d_model), BF16),
                        pltpu.VMEM((conv_ch // LANES, PRE_PITCH * (tt + 2 * HALO), LANES), F32)],
        compiler_params=_params(("parallel", "arbitrary")),
        name="dn_in",
    )(h1, h1, h1, norm_w[1, 0][None], w_in[:, :conv_ch].astype(BF16),
      w_in[:, conv_ch:conv_ch + v_width].astype(BF16), w_ab_nat, w_ab_t, dn_conv_w[0],
      lane_pad(dn_a_log[0]), lane_pad(dn_dt_bias[0]),
      dn_a_log[0].reshape(2 * nheads, 1), dn_dt_bias[0].reshape(2 * nheads, 1), lmat, umat)

    assert tf % DELTA_BLOCK == 0 and DELTA_BLOCK % CHUNK == 0
    nch = tf // DELTA_BLOCK

    def delta_direction(reverse):
        d = int(reverse)
        cidx = (lambda c: nch - 1 - c) if reverse else (lambda c: c)
        ctile = lambda w: pl.BlockSpec((1, DELTA_BLOCK, w), lambda b, c: (b, cidx(c), 0))
        return pl.pallas_call(
            functools.partial(_delta_kernel, nheads=nheads, reverse=reverse),
            grid=(bsz, nch),
            in_specs=[ctile(qk_width), ctile(qk_width), ctile(v_width),
                      pl.BlockSpec((1, 1, DELTA_BLOCK, LANES), lambda b, c: (d, b, cidx(c), 0)),
                      pl.BlockSpec((1, nheads, DELTA_BLOCK), lambda b, c: (b, d, cidx(c)))],
            out_specs=ctile(v_width),
            out_shape=jax.ShapeDtypeStruct((bsz, tf, v_width), ACT),
            scratch_shapes=[pltpu.VMEM((nheads, DN_HEAD_DIM, DN_HEAD_DIM), F32)],
            compiler_params=_params(("parallel", "arbitrary")),
            name="delta_rev" if reverse else "delta_fwd",
        )(q, k, v, gb, gct)

    o_fwd = delta_direction(False)
    o_rev = delta_direction(True)

    xtile = lambda w: pl.BlockSpec((1, tt, w), lambda b, i: (b, i + 1, 0))
    out = pl.pallas_call(
        _dn_out_kernel,
        grid=(bsz, nt - 1),
        in_specs=[xtile(v_width), xtile(v_width), xtile(v_width), xtile(d_model), _const_spec((1, DN_HEAD_DIM)),
                  _const_spec((v_width, d_model)), _const_spec((1, d_model))],
        out_specs=pl.BlockSpec((1, tt, d_model), lambda b, i: (b, i, 0)),
        out_shape=jax.ShapeDtypeStruct((bsz, seq, d_model), F32),
        scratch_shapes=[pltpu.VMEM((tt, v_width), BF16)],
        compiler_params=_params(("parallel", "parallel")),
        name="dn_out",
    )(o_fwd, o_rev, z, h1, dn_norm_w[0][None], dn_w_out[0].astype(BF16), norm_w[1, 1][None])
    return out
```
